```python
import math
import jax, jax.numpy as jnp
from jax import lax
import numpy as np

D_MODEL = 1024
BATCH = 8
SEQ = 2048
DEPTH = 4

RET_HEADS = 4
RET_DK = 128
RET_DV = 256
RET_QK = RET_HEADS * RET_DK
RET_V = RET_HEADS * RET_DV
RET_CHUNK = 128
MOBA_HEADS = 8
MOBA_DH = 128
MOBA_W = MOBA_HEADS * MOBA_DH
MOBA_BLOCK = 256
MOBA_TOPK = 3
MOBA_QCHUNK = 16
IN_WIDTHS = (RET_QK, RET_QK, RET_V, RET_V, MOBA_W, MOBA_W, MOBA_W, MOBA_W, D_MODEL, D_MODEL)
D_IN = sum(IN_WIDTHS)
NORM_EPS = 1e-6

kernel_name = "hybrid_retention_moba_gated"


def rmsnorm(x, w):
    xf = x.astype(jnp.float32)
    xf = xf * lax.rsqrt(jnp.mean(xf * xf, axis=-1, keepdims=True) + NORM_EPS)
    return (xf * w.astype(jnp.float32)).astype(x.dtype)


def split_heads(t, n):
    b, s, _ = t.shape
    return t.reshape(b, s, n, -1).transpose(0, 2, 1, 3)


def merge_heads(t):
    b, n, s, d = t.shape
    return t.transpose(0, 2, 1, 3).reshape(b, s, n * d)


def retention(q, k, v):
    b, h, s, dk = q.shape
    dv = v.shape[-1]
    c = RET_CHUNK
    nc = s // c
    q = q.astype(jnp.float32)
    k = k.astype(jnp.float32) * (dk ** -0.5)
    v = v.astype(jnp.float32)
    log_g = jnp.log1p(-jnp.exp2(-5.0 - jnp.arange(h, dtype=jnp.float32)))
    n = jnp.arange(c, dtype=jnp.float32)
    rel = n[:, None] - n[None, :]
    decay_in = jnp.where(rel[None] >= 0, jnp.exp(jnp.maximum(rel, 0.0)[None] * log_g[:, None, None]), 0.0)
    xi = jnp.exp((n + 1.0)[None, :] * log_g[:, None])[None, :, :, None]
    zeta = jnp.exp((c - 1.0 - n)[None, :] * log_g[:, None])[None, :, :, None]
    g_chunk = jnp.exp(c * log_g)[None, :, None, None]

    def to_chunks(t):
        return jnp.moveaxis(t.reshape(b, h, nc, c, t.shape[-1]), 2, 0)

    def step(state, xs):
        qi, ki, vi = xs
        inner = jnp.einsum('bhnd,bhmd->bhnm', qi, ki) * decay_in
        o = jnp.einsum('bhnm,bhmv->bhnv', inner, vi) + jnp.einsum('bhnd,bhdv->bhnv', qi * xi, state)
        state = g_chunk * state + jnp.einsum('bhmd,bhmv->bhdv', ki * zeta, vi)
        return state, o

    state0 = jnp.zeros((b, h, dk, dv), jnp.float32)
    _, o = lax.scan(step, state0, (to_chunks(q), to_chunks(k), to_chunks(v)))
    return jnp.moveaxis(o, 0, 2).reshape(b, h, s, dv)


def head_groupnorm(o, w):
    mu = jnp.mean(o, axis=-1, keepdims=True)
    var = jnp.mean(jnp.square(o - mu), axis=-1, keepdims=True)
    o = (o - mu) * lax.rsqrt(var + NORM_EPS)
    return merge_heads(o) * w.astype(jnp.float32)


def moba_attention(q, k, v):
    b, h, s, dh = q.shape
    bs = MOBA_BLOCK
    nb = -(-s // bs)
    sp = nb * bs
    if sp > s:
        pad = ((0, 0), (0, 0), (0, sp - s), (0, 0))
        q, k, v = jnp.pad(q, pad), jnp.pad(k, pad), jnp.pad(v, pad)
    scale = dh ** -0.5
    topk = min(MOBA_TOPK, nb)
    slopes = jnp.exp2(-8.0 * (jnp.arange(h, dtype=jnp.float32) + 1.0) / h)
    kb = k.reshape(b, h, nb, bs, dh)
    vb = v.reshape(b, h, nb, bs, dh)
    kmean = jnp.mean(kb.astype(jnp.float32), axis=3)
    gate = jnp.einsum('bhtd,bhjd->bhtj', q.astype(jnp.float32), kmean)
    qblk = jnp.arange(sp) // bs
    past = jnp.arange(nb)[None, :] < qblk[:, None]
    gate = jnp.where(past[None, None], gate, -jnp.inf)
    _, idx = lax.top_k(gate, topk)
    valid = jnp.arange(topk)[None, :] < jnp.minimum(qblk, topk)[:, None]

    nq = sp // MOBA_QCHUNK
    qc_all = jnp.moveaxis(q.reshape(b, h, nq, MOBA_QCHUNK, dh), 2, 0)
    ic_all = jnp.moveaxis(idx.reshape(b, h, nq, MOBA_QCHUNK, topk), 2, 0)
    vc_all = valid.reshape(nq, MOBA_QCHUNK, topk)
    bi = jnp.arange(b)[:, None, None, None]
    hi = jnp.arange(h)[None, :, None, None]
    koff = jnp.arange(bs)

    def chunk(xs):
        qc, ic, vc, cid = xs
        tpos = cid * MOBA_QCHUNK + jnp.arange(MOBA_QCHUNK)
        blk = (cid * MOBA_QCHUNK) // bs
        ks = kb[bi, hi, ic]
        vs = vb[bi, hi, ic]
        s_sel = jnp.einsum('bhqd,bhqjkd->bhqjk', qc, ks).astype(jnp.float32) * scale
        dist_sel = (tpos[None, None, :, None, None] - (ic[..., None] * bs + koff)).astype(jnp.float32)
        s_sel = jnp.where(vc[None, None, :, :, None], s_sel - slopes[None, :, None, None, None] * dist_sel, -jnp.inf)
        s_sel = s_sel.reshape(b, h, MOBA_QCHUNK, topk * bs)
        k_own = lax.dynamic_index_in_dim(kb, blk, axis=2, keepdims=False)
        v_own = lax.dynamic_index_in_dim(vb, blk, axis=2, keepdims=False)
        s_own = jnp.einsum('bhqd,bhkd->bhqk', qc, k_own).astype(jnp.float32) * scale
        dist_own = tpos[:, None] - (blk * bs + koff)[None, :]
        s_own = jnp.where(dist_own[None, None] >= 0,
                          s_own - slopes[None, :, None, None] * dist_own.astype(jnp.float32)[None, None], -jnp.inf)
        p = jax.nn.softmax(jnp.concatenate([s_sel, s_own], axis=-1), axis=-1).astype(v.dtype)
        p_sel, p_own = p[..., :topk * bs], p[..., topk * bs:]
        o = jnp.einsum('bhqn,bhqnd->bhqd', p_sel, vs.reshape(b, h, MOBA_QCHUNK, topk * bs, dh))
        return o + jnp.einsum('bhqk,bhkd->bhqd', p_own, v_own)

    o = lax.map(chunk, (qc_all, ic_all, vc_all, jnp.arange(nq)))
    return jnp.moveaxis(o, 0, 2).reshape(b, h, sp, dh)[:, :, :s]


def hybrid_layer(x, ln_w, w_in, ret_gn_w, w_ret_o, w_moba_o, w_out):
    h = rmsnorm(x, ln_w)
    proj = jnp.einsum('bsd,de->bse', h, w_in)
    splits = [int(o) for o in np.cumsum(IN_WIDTHS)[:-1]]
    rq, rk, rv, rg, mq, mk, mv, mg, gr, gm = jnp.split(proj, splits, axis=-1)
    ret = retention(split_heads(rq, RET_HEADS), split_heads(rk, RET_HEADS), split_heads(rv, RET_HEADS))
    ret = head_groupnorm(ret, ret_gn_w).astype(x.dtype) * jax.nn.silu(rg)
    ret = jnp.einsum('bsv,vd->bsd', ret, w_ret_o)
    mo = moba_attention(split_heads(mq, MOBA_HEADS), split_heads(mk, MOBA_HEADS), split_heads(mv, MOBA_HEADS))
    mo = merge_heads(mo) * jax.nn.silu(mg)
    mo = jnp.einsum('bsv,vd->bsd', mo, w_moba_o)
    y = jax.nn.sigmoid(gr) * ret + jax.nn.sigmoid(gm) * mo
    return x + jnp.einsum('bsd,de->bse', y, w_out)


def setup_inputs(seed: int = 0) -> dict:
    key = jax.random.key(seed)
    ks = jax.random.split(key, 9)
    d = D_MODEL
    x = jax.random.normal(ks[0], (BATCH, SEQ, d), jnp.float32)
    ln_w = 1.0 + 0.02 * jax.random.normal(ks[1], (DEPTH, d), jnp.float32)
    w_in = jax.random.normal(ks[2], (DEPTH, d, D_IN), jnp.float32) * d ** -0.5
    ret_gn_w = 1.0 + 0.02 * jax.random.normal(ks[3], (DEPTH, RET_V), jnp.float32)
    w_ret_o = jax.random.normal(ks[4], (DEPTH, RET_V, d), jnp.float32) * RET_V ** -0.5
    w_moba_o = jax.random.normal(ks[5], (DEPTH, MOBA_W, d), jnp.float32) * MOBA_W ** -0.5
    w_out = jax.random.normal(ks[6], (DEPTH, d, d), jnp.float32) * d ** -0.5
    final_norm_w = 1.0 + 0.02 * jax.random.normal(ks[7], (d,), jnp.float32)
    return {"x": x, "ln_w": ln_w, "w_in": w_in, "ret_gn_w": ret_gn_w, "w_ret_o": w_ret_o,
            "w_moba_o": w_moba_o, "w_out": w_out, "final_norm_w": final_norm_w}


def reference(x, ln_w, w_in, ret_gn_w, w_ret_o, w_moba_o, w_out, final_norm_w):
    for layer in range(DEPTH):
        x = hybrid_layer(x, ln_w[layer], w_in[layer], ret_gn_w[layer], w_ret_o[layer],
                         w_moba_o[layer], w_out[layer])
    return rmsnorm(x, final_norm_w)
```

```python
import functools

import numpy as np
import jax
import jax.numpy as jnp
from jax import lax
from jax.experimental import pallas as pl
from jax.experimental.pallas import tpu as pltpu

D_MODEL = 1024
RET_HEADS = 4
RET_DK = 128
RET_DV = 256
RET_CHUNK = 128
MOBA_HEADS = 8
MOBA_DH = 128
MOBA_BLOCK = 256
MOBA_TOPK = 3
NORM_EPS = 1e-6

OFF_RQ, OFF_RK, OFF_RV, OFF_RG = 0, 512, 1024, 2048
OFF_MQ, OFF_MK, OFF_MV, OFF_MG = 3072, 4096, 5120, 6144
OFF_GR, OFF_GM = 7168, 8192
D_IN = 9216

IN_TM = 1024
IN_TN = 1024
OUT_TM = 512

F32 = jnp.float32
BF16 = jnp.bfloat16

_NT = (((1,), (1,)), ((), ()))
_TN = (((0,), (0,)), ((), ()))


def _in_proj_kernel(x_ref, lnw_ref, w_ref, o_ref, h_ref):
    @pl.when(pl.program_id(1) == 0)
    def _():
        x = x_ref[...]
        ms = jnp.mean(x * x, axis=-1, keepdims=True)
        h_ref[...] = (x * lax.rsqrt(ms + NORM_EPS) * lnw_ref[...]).astype(BF16)

    o_ref[...] = jnp.dot(h_ref[...], w_ref[...], preferred_element_type=F32).astype(o_ref.dtype)


def _in_proj(x2, ln_w, w_in_bf16):
    m = x2.shape[0]
    return pl.pallas_call(
        _in_proj_kernel,
        out_shape=jax.ShapeDtypeStruct((m, D_IN), BF16),
        grid=(m // IN_TM, D_IN // IN_TN),
        in_specs=[
            pl.BlockSpec((IN_TM, D_MODEL), lambda i, j: (i, 0)),
            pl.BlockSpec((1, D_MODEL), lambda i, j: (0, 0)),
            pl.BlockSpec((D_MODEL, IN_TN), lambda i, j: (0, j)),
        ],
        out_specs=pl.BlockSpec((IN_TM, IN_TN), lambda i, j: (i, j)),
        scratch_shapes=[pltpu.VMEM((IN_TM, D_MODEL), BF16)],
        compiler_params=pltpu.CompilerParams(
            dimension_semantics=("parallel", "arbitrary"),
            vmem_limit_bytes=48 * 1024 * 1024,
        ),
        name="in_proj",
    )(x2, ln_w.reshape(1, D_MODEL), w_in_bf16)


def _retention_tables():
    c = RET_CHUNK
    hh = np.arange(RET_HEADS, dtype=np.float64)
    log_g = np.log1p(-np.exp2(-5.0 - hh))
    n = np.arange(c, dtype=np.float64)
    rel = n[:, None] - n[None, :]
    scale = RET_DK ** -0.5
    dec = np.where(rel[None] >= 0, np.exp(np.maximum(rel, 0.0)[None] * log_g[:, None, None]), 0.0) * scale
    xi = np.exp((n + 1.0)[None, :] * log_g[:, None])
    zeta = np.exp((c - 1.0 - n)[None, :] * log_g[:, None]) * scale
    gch = np.exp(c * log_g)
    xi_b = np.broadcast_to(xi[:, :, None], (RET_HEADS, c, RET_DV))
    zeta_b = np.broadcast_to(zeta[:, :, None], (RET_HEADS, c, RET_DK))
    gch_b = np.broadcast_to(gch[:, None, None], (RET_HEADS, 1, RET_DV))
    f = lambda a: jnp.asarray(np.ascontiguousarray(a), dtype=F32)
    return f(dec), f(xi_b), f(zeta_b), f(gch_b)


def _retention_kernel(q_ref, k_ref, v_ref, g_ref, gnw_ref, dec_ref, xi_ref, zeta_ref, gch_ref,
                      o_ref, state_ref):
    c = RET_CHUNK
    n_chunks = q_ref.shape[0] // c
    state_ref[...] = jnp.zeros_like(state_ref)

    def body(ci, carry):
        rows = pl.ds(pl.multiple_of(ci * c, c), c)
        qc = q_ref[rows, :]
        kc = k_ref[rows, :]
        vc = v_ref[rows, :]
        st = state_ref[...]
        inner = lax.dot_general(qc, kc, _NT, preferred_element_type=F32) * dec_ref[0]
        o = jnp.dot(inner.astype(BF16), vc, preferred_element_type=F32)
        o = o + jnp.dot(qc, st.astype(BF16), preferred_element_type=F32) * xi_ref[0]
        kz = (kc.astype(F32) * zeta_ref[0]).astype(BF16)
        state_ref[...] = st * gch_ref[0] + lax.dot_general(kz, vc, _TN, preferred_element_type=F32)
        mu = jnp.mean(o, axis=-1, keepdims=True)
        d = o - mu
        var = jnp.mean(d * d, axis=-1, keepdims=True)
        y = d * lax.rsqrt(var + NORM_EPS) * gnw_ref[...]
        g = g_ref[rows, :].astype(F32)
        o_ref[rows, :] = (y * (g * jax.nn.sigmoid(g))).astype(o_ref.dtype)
        return carry

    lax.fori_loop(0, n_chunks, body, 0)


def _retention(proj, gn_w, tables, batch, seq):
    dec, xi_b, zeta_b, gch_b = tables
    c = RET_CHUNK
    qb, kb = OFF_RQ // RET_DK, OFF_RK // RET_DK
    vb, gb = OFF_RV // RET_DV, OFF_RG // RET_DV
    return pl.pallas_call(
        _retention_kernel,
        out_shape=jax.ShapeDtypeStruct((batch * seq, RET_HEADS * RET_DV), BF16),
        grid=(batch, RET_HEADS),
        in_specs=[
            pl.BlockSpec((seq, RET_DK), lambda b, h: (b, qb + h)),
            pl.BlockSpec((seq, RET_DK), lambda b, h: (b, kb + h)),
            pl.BlockSpec((seq, RET_DV), lambda b, h: (b, vb + h)),
            pl.BlockSpec((seq, RET_DV), lambda b, h: (b, gb + h)),
            pl.BlockSpec((1, RET_DV), lambda b, h: (0, h)),
            pl.BlockSpec((1, c, c), lambda b, h: (h, 0, 0)),
            pl.BlockSpec((1, c, RET_DV), lambda b, h: (h, 0, 0)),
            pl.BlockSpec((1, c, RET_DK), lambda b, h: (h, 0, 0)),
            pl.BlockSpec((1, 1, RET_DV), lambda b, h: (h, 0, 0)),
        ],
        out_specs=pl.BlockSpec((seq, RET_DV), lambda b, h: (b, h)),
        scratch_shapes=[pltpu.VMEM((RET_DK, RET_DV), F32)],
        compiler_params=pltpu.CompilerParams(dimension_semantics=("parallel", "parallel")),
        name="retention",
    )(proj, proj, proj, proj, gn_w.reshape(1, RET_HEADS * RET_DV), dec, xi_b, zeta_b, gch_b)


def _moba_tables():
    hh = np.arange(MOBA_HEADS, dtype=np.float64)
    slopes = np.exp2(-8.0 * (hh + 1.0) / MOBA_HEADS)
    pos = np.arange(MOBA_BLOCK, dtype=np.float64)
    rel = pos[None, :] - pos[:, None]
    relb = -slopes[:, None, None] * rel[None]
    blk = np.broadcast_to((slopes * MOBA_BLOCK)[:, None, None], (MOBA_HEADS, 1, MOBA_BLOCK))
    f = lambda a: jnp.asarray(np.ascontiguousarray(a), dtype=F32)
    return f(relb), f(blk)


def _moba_kernel(q_ref, k_ref, v_ref, g_ref, relb_ref, blk_ref, o_ref, km_ref, sel_ref):
    bs = MOBA_BLOCK
    nb = k_ref.shape[0] // bs
    i = pl.program_id(2)
    scale = MOBA_DH ** -0.5

    @pl.when(i == 0)
    def _():
        for j in range(nb):
            kb = k_ref[j * bs:(j + 1) * bs, :].astype(F32)
            km_ref[j:j + 1, :] = jnp.sum(kb, axis=0, keepdims=True) * (1.0 / bs)

    q = q_ref[...]

    km = km_ref[...]
    km_hi = km.astype(BF16)
    km_lo = (km - km_hi.astype(F32)).astype(BF16)
    gate2 = lax.dot_general(jnp.concatenate([km_hi, km_lo], axis=0), q, _NT,
                            preferred_element_type=F32)
    gate = gate2[:nb] + gate2[nb:]
    jidx = lax.broadcasted_iota(jnp.int32, (nb, bs), 0)
    past = jidx < i
    for j in range(nb):
        row = gate[j:j + 1, :]
        beats = ((gate > row) | ((gate == row) & (jidx < j))) & past
        rank = jnp.sum(beats.astype(F32), axis=0, keepdims=True)
        sel_ref[j:j + 1, :] = (rank < float(MOBA_TOPK)).astype(F32)

    relb = relb_ref[0]

    def scores(j):
        kj = k_ref[pl.ds(pl.multiple_of(j * bs, bs), bs), :]
        return lax.dot_general(kj, q, _NT, preferred_element_type=F32) * scale + relb

    def pv(j, p):
        vj = v_ref[pl.ds(pl.multiple_of(j * bs, bs), bs), :]
        return lax.dot_general(vj, p.astype(BF16), _TN, preferred_element_type=F32)

    kpos = lax.broadcasted_iota(jnp.int32, (bs, bs), 0)
    qpos = lax.broadcasted_iota(jnp.int32, (bs, bs), 1)
    s = jnp.where(kpos <= qpos, scores(i), -jnp.inf)
    m0 = jnp.max(s, axis=0, keepdims=True)
    p = jnp.exp(s - m0)
    l0 = jnp.sum(p, axis=0, keepdims=True)
    acc0 = pv(i, p)

    def body(j, carry):
        m, l, acc = carry
        off = blk_ref[0] * (i - j).astype(F32)
        s = jnp.where(sel_ref[pl.ds(j, 1), :] > 0.5, scores(j) - off, -jnp.inf)
        m_new = jnp.maximum(m, jnp.max(s, axis=0, keepdims=True))
        alpha = jnp.exp(m - m_new)
        p = jnp.exp(s - m_new)
        l = alpha * l + jnp.sum(p, axis=0, keepdims=True)
        acc = acc * alpha + pv(j, p)
        return m_new, l, acc

    m, l, acc = lax.fori_loop(0, i, body, (m0, l0, acc0))
    out = (acc / l).T
    g = g_ref[...].astype(F32)
    o_ref[...] = (out * (g * jax.nn.sigmoid(g))).astype(o_ref.dtype)


def _moba(proj, tables, batch, seq):
    relb, blk = tables
    bs = MOBA_BLOCK
    nb = seq // bs
    qb, kb, vb, gb = (OFF_MQ // MOBA_DH, OFF_MK // MOBA_DH, OFF_MV // MOBA_DH, OFF_MG // MOBA_DH)
    return pl.pallas_call(
        _moba_kernel,
        out_shape=jax.ShapeDtypeStruct((batch * seq, MOBA_HEADS * MOBA_DH), BF16),
        grid=(batch, MOBA_HEADS, nb),
        in_specs=[
            pl.BlockSpec((bs, MOBA_DH), lambda b, h, i: (b * nb + i, qb + h)),
            pl.BlockSpec((seq, MOBA_DH), lambda b, h, i: (b, kb + h)),
            pl.BlockSpec((seq, MOBA_DH), lambda b, h, i: (b, vb + h)),
            pl.BlockSpec((bs, MOBA_DH), lambda b, h, i: (b * nb + i, gb + h)),
            pl.BlockSpec((1, bs, bs), lambda b, h, i: (h, 0, 0)),
            pl.BlockSpec((1, 1, bs), lambda b, h, i: (h, 0, 0)),
        ],
        out_specs=pl.BlockSpec((bs, MOBA_DH), lambda b, h, i: (b * nb + i, h)),
        scratch_shapes=[pltpu.VMEM((nb, MOBA_DH), F32), pltpu.VMEM((nb, bs), F32)],
        compiler_params=pltpu.CompilerParams(
            dimension_semantics=("parallel", "parallel", "arbitrary")),
        name="moba",
    )(proj, proj, proj, proj, relb, blk)


def _out_proj_kernel(ret_ref, mo_ref, gr_ref, gm_ref, x_ref, wr_ref, wm_ref, wo_ref, fnw_ref,
                     o_ref, *, final_norm):
    r = jnp.dot(ret_ref[...], wr_ref[...], preferred_element_type=F32)
    m = jnp.dot(mo_ref[...], wm_ref[...], preferred_element_type=F32)
    y = jax.nn.sigmoid(gr_ref[...].astype(F32)) * r + jax.nn.sigmoid(gm_ref[...].astype(F32)) * m
    out = x_ref[...] + jnp.dot(y.astype(BF16), wo_ref[...], preferred_element_type=F32)
    if final_norm:
        ms = jnp.mean(out * out, axis=-1, keepdims=True)
        out = out * lax.rsqrt(ms + NORM_EPS) * fnw_ref[...]
    o_ref[...] = out


def _out_proj(ret_g, mo_g, proj, x2, w_ret_o, w_moba_o, w_out, final_norm_w, final_norm):
    m = x2.shape[0]
    d = D_MODEL
    tile = lambda col: pl.BlockSpec((OUT_TM, d), lambda i: (i, col))
    full = pl.BlockSpec((d, d), lambda i: (0, 0))
    return pl.pallas_call(
        functools.partial(_out_proj_kernel, final_norm=final_norm),
        out_shape=jax.ShapeDtypeStruct((m, d), F32),
        grid=(m // OUT_TM,),
        in_specs=[tile(0), tile(0), tile(OFF_GR // d), tile(OFF_GM // d), tile(0),
                  full, full, full, pl.BlockSpec((1, d), lambda i: (0, 0))],
        out_specs=tile(0),
        compiler_params=pltpu.CompilerParams(
            dimension_semantics=("parallel",), vmem_limit_bytes=48 * 1024 * 1024),
        name="out_proj",
    )(ret_g, mo_g, proj, proj, x2, w_ret_o, w_moba_o, w_out, final_norm_w.reshape(1, d))


def kernel(x, ln_w, w_in, ret_gn_w, w_ret_o, w_moba_o, w_out, final_norm_w):
    batch, seq, d = x.shape
    depth = w_in.shape[0]
    assert d == D_MODEL and w_in.shape[2] == D_IN
    assert seq % MOBA_BLOCK == 0 and seq % RET_CHUNK == 0 and (batch * seq) % IN_TM == 0
    ret_tables = _retention_tables()
    moba_tables = _moba_tables()
    w_in_b = w_in.astype(BF16)
    w_ret_b = w_ret_o.astype(BF16)
    w_moba_b = w_moba_o.astype(BF16)
    w_out_b = w_out.astype(BF16)
    x2 = x.reshape(batch * seq, d)
    for layer in range(depth):
        proj = _in_proj(x2, ln_w[layer], w_in_b[layer])
        ret_g = _retention(proj, ret_gn_w[layer], ret_tables, batch, seq)
        mo_g = _moba(proj, moba_tables, batch, seq)
        x2 = _out_proj(ret_g, mo_g, proj, x2, w_ret_b[layer], w_moba_b[layer], w_out_b[layer],
                       final_norm_w, final_norm=(layer == depth - 1))
    return x2.reshape(batch, seq, d)
```

```python
import functools

import numpy as np
import jax
import jax.numpy as jnp
from jax import lax
from jax.experimental import pallas as pl
from jax.experimental.pallas import tpu as pltpu

D_MODEL = 1024
RET_HEADS = 4
RET_DK = 128
RET_DV = 256
RET_CHUNK = 128
MOBA_HEADS = 8
MOBA_DH = 128
MOBA_BLOCK = 256
MOBA_TOPK = 3
NORM_EPS = 1e-6

OFF_RQ, OFF_RK, OFF_RV, OFF_RG = 0, 512, 1024, 2048
OFF_MQ, OFF_MK, OFF_MV, OFF_MG = 3072, 4096, 5120, 6144
OFF_GR, OFF_GM = 7168, 8192
D_IN = 9216

IN_TM = 1024
IN_TN = 1024
OUT_TM = 512

F32 = jnp.float32
BF16 = jnp.bfloat16

_NT = (((1,), (1,)), ((), ()))
_TN = (((0,), (0,)), ((), ()))


def _in_proj_kernel(x_ref, lnw_ref, w_ref, o_ref, h_ref):
    @pl.when(pl.program_id(1) == 0)
    def _():
        x = x_ref[...]
        ms = jnp.mean(x * x, axis=-1, keepdims=True)
        h_ref[...] = (x * lax.rsqrt(ms + NORM_EPS) * lnw_ref[...]).astype(BF16)

    o_ref[...] = jnp.dot(h_ref[...], w_ref[...], preferred_element_type=F32).astype(o_ref.dtype)


def _in_proj(x2, ln_w, w_in_bf16):
    m = x2.shape[0]
    return pl.pallas_call(
        _in_proj_kernel,
        out_shape=jax.ShapeDtypeStruct((m, D_IN), BF16),
        grid=(m // IN_TM, D_IN // IN_TN),
        in_specs=[
            pl.BlockSpec((IN_TM, D_MODEL), lambda i, j: (i, 0)),
            pl.BlockSpec((1, D_MODEL), lambda i, j: (0, 0)),
            pl.BlockSpec((D_MODEL, IN_TN), lambda i, j: (0, j)),
        ],
        out_specs=pl.BlockSpec((IN_TM, IN_TN), lambda i, j: (i, j)),
        scratch_shapes=[pltpu.VMEM((IN_TM, D_MODEL), BF16)],
        compiler_params=pltpu.CompilerParams(
            dimension_semantics=("parallel", "arbitrary"),
            vmem_limit_bytes=48 * 1024 * 1024,
        ),
        name="in_proj",
    )(x2, ln_w.reshape(1, D_MODEL), w_in_bf16)


def _retention_tables():
    c = RET_CHUNK
    hh = np.arange(RET_HEADS, dtype=np.float64)
    log_g = np.log1p(-np.exp2(-5.0 - hh))
    n = np.arange(c, dtype=np.float64)
    rel = n[:, None] - n[None, :]
    scale = RET_DK ** -0.5
    dec = np.where(rel[None] >= 0, np.exp(np.maximum(rel, 0.0)[None] * log_g[:, None, None]), 0.0) * scale
    xi = np.exp((n + 1.0)[None, :] * log_g[:, None])
    zeta = np.exp((c - 1.0 - n)[None, :] * log_g[:, None]) * scale
    gch = np.exp(c * log_g)
    xi_b = np.broadcast_to(xi[:, :, None], (RET_HEADS, c, RET_DV))
    zeta_b = np.broadcast_to(zeta[:, :, None], (RET_HEADS, c, RET_DK))
    gch_b = np.broadcast_to(gch[:, None, None], (RET_HEADS, 1, RET_DV))
    f = lambda a: jnp.asarray(np.ascontiguousarray(a), dtype=F32)
    return f(dec), f(xi_b), f(zeta_b), f(gch_b)


def _retention_kernel(q_ref, k_ref, v_ref, g_ref, gnw_ref, dec_ref, xi_ref, zeta_ref, gch_ref,
                      o_ref, state_ref):
    c = RET_CHUNK
    n_chunks = q_ref.shape[0] // c
    state_ref[...] = jnp.zeros_like(state_ref)

    def body(ci, carry):
        rows = pl.ds(pl.multiple_of(ci * c, c), c)
        qc = q_ref[rows, :]
        kc = k_ref[rows, :]
        vc = v_ref[rows, :]
        st = state_ref[...]
        inner = lax.dot_general(qc, kc, _NT, preferred_element_type=F32) * dec_ref[0]
        o = jnp.dot(inner.astype(BF16), vc, preferred_element_type=F32)
        o = o + jnp.dot(qc, st.astype(BF16), preferred_element_type=F32) * xi_ref[0]
        kz = (kc.astype(F32) * zeta_ref[0]).astype(BF16)
        state_ref[...] = st * gch_ref[0] + lax.dot_general(kz, vc, _TN, preferred_element_type=F32)
        mu = jnp.mean(o, axis=-1, keepdims=True)
        d = o - mu
        var = jnp.mean(d * d, axis=-1, keepdims=True)
        y = d * lax.rsqrt(var + NORM_EPS) * gnw_ref[...]
        g = g_ref[rows, :].astype(F32)
        o_ref[rows, :] = (y * (g * jax.nn.sigmoid(g))).astype(o_ref.dtype)
        return carry

    lax.fori_loop(0, n_chunks, body, 0)


def _retention(proj, gn_w, tables, batch, seq):
    dec, xi_b, zeta_b, gch_b = tables
    c = RET_CHUNK
    qb, kb = OFF_RQ // RET_DK, OFF_RK // RET_DK
    vb, gb = OFF_RV // RET_DV, OFF_RG // RET_DV
    return pl.pallas_call(
        _retention_kernel,
        out_shape=jax.ShapeDtypeStruct((batch * seq, RET_HEADS * RET_DV), BF16),
        grid=(batch, RET_HEADS),
        in_specs=[
            pl.BlockSpec((seq, RET_DK), lambda b, h: (b, qb + h)),
            pl.BlockSpec((seq, RET_DK), lambda b, h: (b, kb + h)),
            pl.BlockSpec((seq, RET_DV), lambda b, h: (b, vb + h)),
            pl.BlockSpec((seq, RET_DV), lambda b, h: (b, gb + h)),
            pl.BlockSpec((1, RET_DV), lambda b, h: (0, h)),
            pl.BlockSpec((1, c, c), lambda b, h: (h, 0, 0)),
            pl.BlockSpec((1, c, RET_DV), lambda b, h: (h, 0, 0)),
            pl.BlockSpec((1, c, RET_DK), lambda b, h: (h, 0, 0)),
            pl.BlockSpec((1, 1, RET_DV), lambda b, h: (h, 0, 0)),
        ],
        out_specs=pl.BlockSpec((seq, RET_DV), lambda b, h: (b, h)),
        scratch_shapes=[pltpu.VMEM((RET_DK, RET_DV), F32)],
        compiler_params=pltpu.CompilerParams(dimension_semantics=("parallel", "parallel")),
        name="retention",
    )(proj, proj, proj, proj, gn_w.reshape(1, RET_HEADS * RET_DV), dec, xi_b, zeta_b, gch_b)


def _moba_tables(seq):
    slopes = np.exp2(-8.0 * (np.arange(MOBA_HEADS, dtype=np.float64) + 1.0) / MOBA_HEADS)
    a = jnp.asarray(slopes * (MOBA_DH ** 0.5), dtype=F32)
    pos = jnp.arange(seq, dtype=F32)
    return jnp.broadcast_to(a[:, None, None] * pos[None, :, None], (MOBA_HEADS, seq, MOBA_DH))


def _moba_kernel(q_ref, k_ref, v_ref, g_ref, kbias_ref, o_ref, s_ref):
    bs = MOBA_BLOCK
    nb = k_ref.shape[0] // bs
    exp2_scale = (MOBA_DH ** -0.5) * float(np.log2(np.e))
    blk = lambda j: slice(j * bs, (j + 1) * bs)

    km = jnp.concatenate(
        [jnp.sum(k_ref[blk(j), :].astype(F32), axis=0, keepdims=True) for j in range(nb)], axis=0) * (1.0 / bs)
    km_hi = km.astype(BF16)
    km_lo = (km - km_hi.astype(F32)).astype(BF16)
    km_hl = jnp.concatenate([km_hi, km_lo], axis=0)

    jidx = lax.broadcasted_iota(jnp.int32, (nb, bs), 0)
    kpos = lax.broadcasted_iota(jnp.int32, (bs, bs), 0)
    qpos = lax.broadcasted_iota(jnp.int32, (bs, bs), 1)
    causal = kpos <= qpos

    for i in range(nb):
        q = q_ref[blk(i), :]
        sel = None
        if i > MOBA_TOPK:
            gate2 = lax.dot_general(km_hl, q, _NT, preferred_element_type=F32)
            gate = gate2[:nb] + gate2[nb:]
            past = jidx < i
            sel = []
            for j in range(i):
                row = gate[j:j + 1, :]
                beats = ((gate > row) | ((gate == row) & (jidx < j))) & past
                rank = jnp.sum(beats.astype(F32), axis=0, keepdims=True)
                sel.append(rank < float(MOBA_TOPK))

        slot = i % 2
        mx = None
        for j in range(i + 1):
            kb = kbias_ref[0, blk(j), :]
            s = lax.dot_general(k_ref[blk(j), :], q, _NT, preferred_element_type=F32)
            s = s + jnp.concatenate([kb, kb], axis=1)
            if j == i:
                s = jnp.where(causal, s, -jnp.inf)
            elif sel is not None:
                s = jnp.where(sel[j], s, -jnp.inf)
            s_ref[slot, j] = s
            mj = jnp.max(s, axis=0, keepdims=True)
            mx = mj if mx is None else jnp.maximum(mx, mj)

        l = None
        acc = None
        for j in range(i + 1):
            p = jnp.exp2((s_ref[slot, j] - mx) * exp2_scale)
            lj = jnp.sum(p, axis=0, keepdims=True)
            aj = lax.dot_general(v_ref[blk(j), :], p.astype(BF16), _TN, preferred_element_type=F32)
            l = lj if l is None else l + lj
            acc = aj if acc is None else acc + aj
        out = (acc / l).T
        g = g_ref[blk(i), :].astype(F32)
        o_ref[blk(i), :] = (out * (g * jax.nn.sigmoid(g))).astype(o_ref.dtype)


def _moba(proj, kbias, batch, seq):
    bs = MOBA_BLOCK
    nb = seq // bs
    qb, kb, vb, gb = (OFF_MQ // MOBA_DH, OFF_MK // MOBA_DH, OFF_MV // MOBA_DH, OFF_MG // MOBA_DH)
    col = lambda base: pl.BlockSpec((seq, MOBA_DH), lambda h, b: (b, base + h))
    return pl.pallas_call(
        _moba_kernel,
        out_shape=jax.ShapeDtypeStruct((batch * seq, MOBA_HEADS * MOBA_DH), BF16),
        grid=(MOBA_HEADS, batch),
        in_specs=[col(qb), col(kb), col(vb), col(gb),
                  pl.BlockSpec((1, seq, MOBA_DH), lambda h, b: (h, 0, 0))],
        out_specs=col(0),
        scratch_shapes=[pltpu.VMEM((2, nb, bs, bs), F32)],
        compiler_params=pltpu.CompilerParams(
            dimension_semantics=("parallel", "parallel"), vmem_limit_bytes=48 * 1024 * 1024),
        name="moba",
    )(proj, proj, proj, proj, kbias)


def _out_proj_kernel(ret_ref, mo_ref, gr_ref, gm_ref, x_ref, wr_ref, wm_ref, wo_ref, fnw_ref,
                     o_ref, *, final_norm):
    r = jnp.dot(ret_ref[...], wr_ref[...], preferred_element_type=F32)
    m = jnp.dot(mo_ref[...], wm_ref[...], preferred_element_type=F32)
    y = jax.nn.sigmoid(gr_ref[...].astype(F32)) * r + jax.nn.sigmoid(gm_ref[...].astype(F32)) * m
    out = x_ref[...] + jnp.dot(y.astype(BF16), wo_ref[...], preferred_element_type=F32)
    if final_norm:
        ms = jnp.mean(out * out, axis=-1, keepdims=True)
        out = out * lax.rsqrt(ms + NORM_EPS) * fnw_ref[...]
    o_ref[...] = out


def _out_proj(ret_g, mo_g, proj, x2, w_ret_o, w_moba_o, w_out, final_norm_w, final_norm):
    m = x2.shape[0]
    d = D_MODEL
    tile = lambda col: pl.BlockSpec((OUT_TM, d), lambda i: (i, col))
    full = pl.BlockSpec((d, d), lambda i: (0, 0))
    return pl.pallas_call(
        functools.partial(_out_proj_kernel, final_norm=final_norm),
        out_shape=jax.ShapeDtypeStruct((m, d), F32),
        grid=(m // OUT_TM,),
        in_specs=[tile(0), tile(0), tile(OFF_GR // d), tile(OFF_GM // d), tile(0),
                  full, full, full, pl.BlockSpec((1, d), lambda i: (0, 0))],
        out_specs=tile(0),
        compiler_params=pltpu.CompilerParams(
            dimension_semantics=("parallel",), vmem_limit_bytes=48 * 1024 * 1024),
        name="out_proj",
    )(ret_g, mo_g, proj, proj, x2, w_ret_o, w_moba_o, w_out, final_norm_w.reshape(1, d))


def kernel(x, ln_w, w_in, ret_gn_w, w_ret_o, w_moba_o, w_out, final_norm_w):
    batch, seq, d = x.shape
    depth = w_in.shape[0]
    assert d == D_MODEL and w_in.shape[2] == D_IN
    assert seq % MOBA_BLOCK == 0 and seq % RET_CHUNK == 0 and (batch * seq) % IN_TM == 0
    ret_tables = _retention_tables()
    moba_kbias = _moba_tables(seq)
    w_in_b = w_in.astype(BF16)
    w_ret_b = w_ret_o.astype(BF16)
    w_moba_b = w_moba_o.astype(BF16)
    w_out_b = w_out.astype(BF16)
    x2 = x.reshape(batch * seq, d)
    for layer in range(depth):
        proj = _in_proj(x2, ln_w[layer], w_in_b[layer])
        ret_g = _retention(proj, ret_gn_w[layer], ret_tables, batch, seq)
        mo_g = _moba(proj, moba_kbias, batch, seq)
        x2 = _out_proj(ret_g, mo_g, proj, x2, w_ret_b[layer], w_moba_b[layer], w_out_b[layer],
                       final_norm_w, final_norm=(layer == depth - 1))
    return x2.reshape(batch, seq, d)
```

```python
import functools

import numpy as np
import jax
import jax.numpy as jnp
from jax import lax
from jax.experimental import pallas as pl
from jax.experimental.pallas import tpu as pltpu

D_MODEL = 1024
RET_HEADS = 4
RET_DK = 128
RET_DV = 256
RET_CHUNK = 256
MOBA_HEADS = 8
MOBA_DH = 128
MOBA_BLOCK = 256
MOBA_TOPK = 3
NORM_EPS = 1e-6

OFF_RQ, OFF_RK, OFF_RV, OFF_RG = 0, 512, 1024, 2048
OFF_MQ, OFF_MK, OFF_MV, OFF_MG = 3072, 4096, 5120, 6144
OFF_GR, OFF_GM = 7168, 8192
D_IN = 9216

NORM_TM = 512
IN_TM = 2048
IN_TN = 1024
OUT_TM = 512

F32 = jnp.float32
BF16 = jnp.bfloat16

_NT = (((1,), (1,)), ((), ()))
_TN = (((0,), (0,)), ((), ()))


def _rmsnorm_kernel(x_ref, w_ref, o_ref):
    x = x_ref[...]
    ms = jnp.mean(x * x, axis=-1, keepdims=True)
    o_ref[...] = (x * lax.rsqrt(ms + NORM_EPS) * w_ref[...]).astype(o_ref.dtype)


def _rmsnorm(x2, ln_w3, layer):
    m, d = x2.shape
    return pl.pallas_call(
        _rmsnorm_kernel,
        out_shape=jax.ShapeDtypeStruct((m, d), BF16),
        grid=(m // NORM_TM,),
        in_specs=[pl.BlockSpec((NORM_TM, d), lambda i: (i, 0)),
                  pl.BlockSpec((None, 1, d), lambda i: (layer, 0, 0))],
        out_specs=pl.BlockSpec((NORM_TM, d), lambda i: (i, 0)),
        compiler_params=pltpu.CompilerParams(dimension_semantics=("parallel",)),
        name="rmsnorm",
    )(x2, ln_w3)


def _in_proj_kernel(h_ref, w_ref, o_ref, wb_ref):
    @pl.when(pl.program_id(1) == 0)
    def _():
        wb_ref[...] = w_ref[...].astype(BF16)

    o_ref[...] = jnp.dot(h_ref[...], wb_ref[...], preferred_element_type=F32).astype(o_ref.dtype)


def _in_proj(h, w_in, layer):
    m = h.shape[0]
    return pl.pallas_call(
        _in_proj_kernel,
        out_shape=jax.ShapeDtypeStruct((m, D_IN), BF16),
        grid=(D_IN // IN_TN, m // IN_TM),
        in_specs=[
            pl.BlockSpec((IN_TM, D_MODEL), lambda j, i: (i, 0)),
            pl.BlockSpec((None, D_MODEL, IN_TN), lambda j, i: (layer, 0, j)),
        ],
        out_specs=pl.BlockSpec((IN_TM, IN_TN), lambda j, i: (i, j)),
        scratch_shapes=[pltpu.VMEM((D_MODEL, IN_TN), BF16)],
        compiler_params=pltpu.CompilerParams(
            dimension_semantics=("parallel", "arbitrary"),
            vmem_limit_bytes=48 * 1024 * 1024,
        ),
        name="in_proj",
    )(h, w_in)


def _retention_tables():
    c = RET_CHUNK
    hh = np.arange(RET_HEADS, dtype=np.float64)
    log_g = np.log1p(-np.exp2(-5.0 - hh))
    n = np.arange(c, dtype=np.float64)
    rel = n[:, None] - n[None, :]
    scale = RET_DK ** -0.5
    dec = np.where(rel[None] >= 0, np.exp(np.maximum(rel, 0.0)[None] * log_g[:, None, None]), 0.0) * scale
    xi = np.exp((n + 1.0)[None, :] * log_g[:, None])
    zeta = np.exp((c - 1.0 - n)[None, :] * log_g[:, None]) * scale
    gch = np.exp(c * log_g)
    xi_b = np.broadcast_to(xi[:, :, None], (RET_HEADS, c, RET_DV))
    zeta_b = np.broadcast_to(zeta[:, :, None], (RET_HEADS, c, RET_DK))
    gch_b = np.broadcast_to(gch[:, None, None], (RET_HEADS, 1, RET_DV))
    f = lambda a: jnp.asarray(np.ascontiguousarray(a), dtype=F32)
    return f(dec), f(xi_b), f(zeta_b), f(gch_b)


def _retention_kernel(q_ref, k_ref, v_ref, g_ref, gnw_ref, dec_ref, xi_ref, zeta_ref, gch_ref, o_ref):
    c = RET_CHUNK
    n_chunks = q_ref.shape[0] // c
    st = None
    for ci in range(n_chunks):
        rows = slice(ci * c, (ci + 1) * c)
        qc = q_ref[rows, :]
        kc = k_ref[rows, :]
        vc = v_ref[rows, :]
        inner = lax.dot_general(qc, kc, _NT, preferred_element_type=F32) * dec_ref[0]
        o = jnp.dot(inner.astype(BF16), vc, preferred_element_type=F32)
        if st is not None:
            o = o + jnp.dot(qc, st.astype(BF16), preferred_element_type=F32) * xi_ref[0]
        if ci + 1 < n_chunks:
            kz = (kc.astype(F32) * zeta_ref[0]).astype(BF16)
            kv = lax.dot_general(kz, vc, _TN, preferred_element_type=F32)
            st = kv if st is None else st * gch_ref[0] + kv
        mu = jnp.mean(o, axis=-1, keepdims=True)
        d = o - mu
        var = jnp.mean(d * d, axis=-1, keepdims=True)
        y = d * lax.rsqrt(var + NORM_EPS) * gnw_ref[...]
        g = g_ref[rows, :].astype(F32)
        o_ref[rows, :] = (y * (g * jax.nn.sigmoid(g))).astype(o_ref.dtype)


def _retention(proj, gn_w3, layer, tables, batch, seq):
    dec, xi_b, zeta_b, gch_b = tables
    c = RET_CHUNK
    qb, kb = OFF_RQ // RET_DK, OFF_RK // RET_DK
    vb, gb = OFF_RV // RET_DV, OFF_RG // RET_DV
    return pl.pallas_call(
        _retention_kernel,
        out_shape=jax.ShapeDtypeStruct((batch * seq, RET_HEADS * RET_DV), BF16),
        grid=(RET_HEADS, batch),
        in_specs=[
            pl.BlockSpec((seq, RET_DK), lambda h, b: (b, qb + h)),
            pl.BlockSpec((seq, RET_DK), lambda h, b: (b, kb + h)),
            pl.BlockSpec((seq, RET_DV), lambda h, b: (b, vb + h)),
            pl.BlockSpec((seq, RET_DV), lambda h, b: (b, gb + h)),
            pl.BlockSpec((None, 1, RET_DV), lambda h, b: (layer, 0, h)),
            pl.BlockSpec((1, c, c), lambda h, b: (h, 0, 0)),
            pl.BlockSpec((1, c, RET_DV), lambda h, b: (h, 0, 0)),
            pl.BlockSpec((1, c, RET_DK), lambda h, b: (h, 0, 0)),
            pl.BlockSpec((1, 1, RET_DV), lambda h, b: (h, 0, 0)),
        ],
        out_specs=pl.BlockSpec((seq, RET_DV), lambda h, b: (b, h)),
        compiler_params=pltpu.CompilerParams(
            dimension_semantics=("parallel", "parallel"), vmem_limit_bytes=48 * 1024 * 1024),
        name="retention",
    )(proj, proj, proj, proj, gn_w3, dec, xi_b, zeta_b, gch_b)


def _moba_tables(seq):
    slopes = np.exp2(-8.0 * (np.arange(MOBA_HEADS, dtype=np.float64) + 1.0) / MOBA_HEADS)
    a = jnp.asarray(slopes * (MOBA_DH ** 0.5), dtype=F32)
    pos = jnp.arange(seq, dtype=F32)
    return jnp.broadcast_to(a[:, None, None] * pos[None, :, None], (MOBA_HEADS, seq, MOBA_DH))


def _moba_kernel(q_ref, k_ref, v_ref, g_ref, kbias_ref, o_ref, s_ref):
    bs = MOBA_BLOCK
    nb = k_ref.shape[0] // bs
    exp2_scale = (MOBA_DH ** -0.5) * float(np.log2(np.e))
    blk = lambda j: slice(j * bs, (j + 1) * bs)

    km = jnp.concatenate(
        [jnp.sum(k_ref[blk(j), :].astype(F32), axis=0, keepdims=True) for j in range(nb)], axis=0) * (1.0 / bs)
    km_hi = km.astype(BF16)
    km_lo = (km - km_hi.astype(F32)).astype(BF16)
    km_hl = jnp.concatenate([km_hi, km_lo], axis=0)

    jidx = lax.broadcasted_iota(jnp.int32, (nb, bs), 0)
    kpos = lax.broadcasted_iota(jnp.int32, (bs, bs), 0)
    qpos = lax.broadcasted_iota(jnp.int32, (bs, bs), 1)
    causal = kpos <= qpos

    for i in range(nb):
        q = q_ref[blk(i), :]
        sel = None
        if i > MOBA_TOPK:
            gate2 = lax.dot_general(km_hl, q, _NT, preferred_element_type=F32)
            gate = gate2[:nb] + gate2[nb:]
            past = jidx < i
            sel = []
            for j in range(i):
                row = gate[j:j + 1, :]
                beats = ((gate > row) | ((gate == row) & (jidx < j))) & past
                rank = jnp.sum(beats.astype(F32), axis=0, keepdims=True)
                sel.append(rank < float(MOBA_TOPK))

        slot = i % 2
        mx = None
        for j in range(i + 1):
            kb = kbias_ref[0, blk(j), :]
            s = lax.dot_general(k_ref[blk(j), :], q, _NT, preferred_element_type=F32)
            s = s + jnp.concatenate([kb, kb], axis=1)
            if j == i:
                s = jnp.where(causal, s, -jnp.inf)
            elif sel is not None:
                s = jnp.where(sel[j], s, -jnp.inf)
            s_ref[slot, j] = s
            mj = jnp.max(s, axis=0, keepdims=True)
            mx = mj if mx is None else jnp.maximum(mx, mj)

        l = None
        acc = None
        for j in range(i + 1):
            p = jnp.exp2((s_ref[slot, j] - mx) * exp2_scale)
            lj = jnp.sum(p, axis=0, keepdims=True)
            aj = lax.dot_general(v_ref[blk(j), :], p.astype(BF16), _TN, preferred_element_type=F32)
            l = lj if l is None else l + lj
            acc = aj if acc is None else acc + aj
        out = (acc / l).T
        g = g_ref[blk(i), :].astype(F32)
        o_ref[blk(i), :] = (out * (g * jax.nn.sigmoid(g))).astype(o_ref.dtype)


def _moba(proj, kbias, batch, seq):
    bs = MOBA_BLOCK
    nb = seq // bs
    qb, kb, vb, gb = (OFF_MQ // MOBA_DH, OFF_MK // MOBA_DH, OFF_MV // MOBA_DH, OFF_MG // MOBA_DH)
    col = lambda base: pl.BlockSpec((seq, MOBA_DH), lambda h, b: (b, base + h))
    return pl.pallas_call(
        _moba_kernel,
        out_shape=jax.ShapeDtypeStruct((batch * seq, MOBA_HEADS * MOBA_DH), BF16),
        grid=(MOBA_HEADS, batch),
        in_specs=[col(qb), col(kb), col(vb), col(gb),
                  pl.BlockSpec((1, seq, MOBA_DH), lambda h, b: (h, 0, 0))],
        out_specs=col(0),
        scratch_shapes=[pltpu.VMEM((2, nb, bs, bs), F32)],
        compiler_params=pltpu.CompilerParams(
            dimension_semantics=("parallel", "parallel"), vmem_limit_bytes=48 * 1024 * 1024),
        name="moba",
    )(proj, proj, proj, proj, kbias)


def _out_proj_kernel(ret_ref, mo_ref, gr_ref, gm_ref, x_ref, wr_ref, wm_ref, wo_ref, nw_ref,
                     *out_refs, last):
    r = jnp.dot(ret_ref[...], wr_ref[...], preferred_element_type=F32)
    m = jnp.dot(mo_ref[...], wm_ref[...], preferred_element_type=F32)
    y = jax.nn.sigmoid(gr_ref[...].astype(F32)) * r + jax.nn.sigmoid(gm_ref[...].astype(F32)) * m
    out = x_ref[...] + jnp.dot(y.astype(BF16), wo_ref[...], preferred_element_type=F32)
    ms = jnp.mean(out * out, axis=-1, keepdims=True)
    normed = out * lax.rsqrt(ms + NORM_EPS) * nw_ref[...]
    if last:
        out_refs[0][...] = normed
    else:
        out_refs[0][...] = out
        out_refs[1][...] = normed.astype(BF16)


def _out_proj(ret_g, mo_g, proj, x2, w_ret_o, w_moba_o, w_out, norm_w3, layer, norm_layer, last):
    m = x2.shape[0]
    d = D_MODEL
    tile = lambda col: pl.BlockSpec((OUT_TM, d), lambda i: (i, col))
    full = pl.BlockSpec((None, d, d), lambda i: (layer, 0, 0))
    x_shape = jax.ShapeDtypeStruct((m, d), F32)
    out_shape = x_shape if last else (x_shape, jax.ShapeDtypeStruct((m, d), BF16))
    out_specs = tile(0) if last else (tile(0), tile(0))
    return pl.pallas_call(
        functools.partial(_out_proj_kernel, last=last),
        out_shape=out_shape,
        grid=(m // OUT_TM,),
        in_specs=[tile(0), tile(0), tile(OFF_GR // d), tile(OFF_GM // d), tile(0),
                  full, full, full, pl.BlockSpec((None, 1, d), lambda i: (norm_layer, 0, 0))],
        out_specs=out_specs,
        compiler_params=pltpu.CompilerParams(
            dimension_semantics=("parallel",), vmem_limit_bytes=48 * 1024 * 1024),
        name="out_proj",
    )(ret_g, mo_g, proj, proj, x2, w_ret_o, w_moba_o, w_out, norm_w3)


def kernel(x, ln_w, w_in, ret_gn_w, w_ret_o, w_moba_o, w_out, final_norm_w):
    batch, seq, d = x.shape
    depth = w_in.shape[0]
    assert d == D_MODEL and w_in.shape[2] == D_IN
    assert seq % MOBA_BLOCK == 0 and seq % RET_CHUNK == 0 and (batch * seq) % IN_TM == 0
    ret_tables = _retention_tables()
    moba_kbias = _moba_tables(seq)
    w_ret_b = w_ret_o.astype(BF16)
    w_moba_b = w_moba_o.astype(BF16)
    w_out_b = w_out.astype(BF16)
    ln_w3 = ln_w.reshape(depth, 1, d)
    gn_w3 = ret_gn_w.reshape(depth, 1, RET_HEADS * RET_DV)
    fn_w3 = final_norm_w.reshape(1, 1, d)
    x2 = x.reshape(batch * seq, d)
    h = _rmsnorm(x2, ln_w3, 0)
    for layer in range(depth):
        last = layer == depth - 1
        proj = _in_proj(h, w_in, layer)
        ret_g = _retention(proj, gn_w3, layer, ret_tables, batch, seq)
        mo_g = _moba(proj, moba_kbias, batch, seq)
        res = _out_proj(ret_g, mo_g, proj, x2, w_ret_b, w_moba_b, w_out_b,
                        fn_w3 if last else ln_w3, layer, 0 if last else layer + 1, last)
        if last:
            x2 = res
        else:
            x2, h = res
    return x2.reshape(batch, seq, d)
```

```python
import functools

import numpy as np
import jax
import jax.numpy as jnp
from jax import lax
from jax.experimental import pallas as pl
from jax.experimental.pallas import tpu as pltpu

D_MODEL = 1024
RET_HEADS = 4
RET_DK = 128
RET_DV = 256
RET_CHUNK = 256
MOBA_HEADS = 8
MOBA_DH = 128
MOBA_BLOCK = 256
MOBA_TOPK = 3
MOBA_EXP2_SCALE = MOBA_DH ** -0.5 * float(np.log2(np.e))
MOBA_AUG = 16
NORM_EPS = 1e-6

OFF_RQ, OFF_RK, OFF_RV, OFF_RG = 0, 512, 1024, 2048
OFF_MQ, OFF_MK, OFF_MV, OFF_MG = 3072, 4096, 5120, 6144
OFF_GR, OFF_GM = 7168, 8192
D_IN = 9216

NORM_TM = 512
IN_TM = 2048
IN_TN = 1024
OUT_TM = 512

F32 = jnp.float32
BF16 = jnp.bfloat16

_NT = (((1,), (1,)), ((), ()))
_TN = (((0,), (0,)), ((), ()))


def _rmsnorm_kernel(x_ref, w_ref, o_ref):
    x = x_ref[...]
    ms = jnp.mean(x * x, axis=-1, keepdims=True)
    o_ref[...] = (x * lax.rsqrt(ms + NORM_EPS) * w_ref[...]).astype(o_ref.dtype)


def _rmsnorm(x2, ln_w3, layer):
    m, d = x2.shape
    return pl.pallas_call(
        _rmsnorm_kernel,
        out_shape=jax.ShapeDtypeStruct((m, d), BF16),
        grid=(m // NORM_TM,),
        in_specs=[pl.BlockSpec((NORM_TM, d), lambda i: (i, 0)),
                  pl.BlockSpec((None, 1, d), lambda i: (layer, 0, 0))],
        out_specs=pl.BlockSpec((NORM_TM, d), lambda i: (i, 0)),
        compiler_params=pltpu.CompilerParams(dimension_semantics=("parallel",)),
        name="rmsnorm",
    )(x2, ln_w3)


def _in_proj_col_scale():
    cs = np.ones((1, D_IN), np.float32)
    cs[0, OFF_MQ:OFF_MQ + MOBA_HEADS * MOBA_DH] = MOBA_EXP2_SCALE
    return jnp.asarray(cs)


def _in_proj_kernel(h_ref, w_ref, cs_ref, o_ref, wb_ref):
    @pl.when(pl.program_id(1) == 0)
    def _():
        wb_ref[...] = (w_ref[...] * cs_ref[...]).astype(BF16)

    o_ref[...] = jnp.dot(h_ref[...], wb_ref[...], preferred_element_type=F32).astype(o_ref.dtype)


def _in_proj(h, w_in, col_scale, layer):
    m = h.shape[0]
    return pl.pallas_call(
        _in_proj_kernel,
        out_shape=jax.ShapeDtypeStruct((m, D_IN), BF16),
        grid=(D_IN // IN_TN, m // IN_TM),
        in_specs=[
            pl.BlockSpec((IN_TM, D_MODEL), lambda j, i: (i, 0)),
            pl.BlockSpec((None, D_MODEL, IN_TN), lambda j, i: (layer, 0, j)),
            pl.BlockSpec((1, IN_TN), lambda j, i: (0, j)),
        ],
        out_specs=pl.BlockSpec((IN_TM, IN_TN), lambda j, i: (i, j)),
        scratch_shapes=[pltpu.VMEM((D_MODEL, IN_TN), BF16)],
        compiler_params=pltpu.CompilerParams(
            dimension_semantics=("parallel", "arbitrary"),
            vmem_limit_bytes=48 * 1024 * 1024,
        ),
        name="in_proj",
    )(h, w_in, col_scale)


def _retention_tables():
    c = RET_CHUNK
    hh = np.arange(RET_HEADS, dtype=np.float64)
    log_g = np.log1p(-np.exp2(-5.0 - hh))
    n = np.arange(c, dtype=np.float64)
    rel = n[:, None] - n[None, :]
    scale = RET_DK ** -0.5
    dec = np.where(rel[None] >= 0, np.exp(np.maximum(rel, 0.0)[None] * log_g[:, None, None]), 0.0) * scale
    xi = np.exp((n + 1.0)[None, :] * log_g[:, None])
    zeta = np.exp((c - 1.0 - n)[None, :] * log_g[:, None]) * scale
    gch = np.exp(c * log_g)
    xi_b = np.broadcast_to(xi[:, :, None], (RET_HEADS, c, RET_DV))
    zeta_b = np.broadcast_to(zeta[:, :, None], (RET_HEADS, c, RET_DK))
    gch_b = np.broadcast_to(gch[:, None, None], (RET_HEADS, 1, RET_DV))
    f = lambda a: jnp.asarray(np.ascontiguousarray(a), dtype=F32)
    return f(dec), f(xi_b), f(zeta_b), f(gch_b)


def _retention_kernel(q_ref, k_ref, v_ref, g_ref, gnw_ref, dec_ref, xi_ref, zeta_ref, gch_ref, o_ref):
    c = RET_CHUNK
    n_chunks = q_ref.shape[0] // c
    st = None
    for ci in range(n_chunks):
        rows = slice(ci * c, (ci + 1) * c)
        qc = q_ref[rows, :]
        kc = k_ref[rows, :]
        vc = v_ref[rows, :]
        inner = lax.dot_general(qc, kc, _NT, preferred_element_type=F32) * dec_ref[0]
        o = jnp.dot(inner.astype(BF16), vc, preferred_element_type=F32)
        if st is not None:
            o = o + jnp.dot(qc, st.astype(BF16), preferred_element_type=F32) * xi_ref[0]
        if ci + 1 < n_chunks:
            kz = (kc.astype(F32) * zeta_ref[0]).astype(BF16)
            kv = lax.dot_general(kz, vc, _TN, preferred_element_type=F32)
            st = kv if st is None else st * gch_ref[0] + kv
        mu = jnp.mean(o, axis=-1, keepdims=True)
        d = o - mu
        var = jnp.mean(d * d, axis=-1, keepdims=True)
        y = d * lax.rsqrt(var + NORM_EPS) * gnw_ref[...]
        g = g_ref[rows, :].astype(F32)
        o_ref[rows, :] = (y * (g * jax.nn.sigmoid(g))).astype(o_ref.dtype)


def _retention(proj, gn_w3, layer, tables, batch, seq):
    dec, xi_b, zeta_b, gch_b = tables
    c = RET_CHUNK
    qb, kb = OFF_RQ // RET_DK, OFF_RK // RET_DK
    vb, gb = OFF_RV // RET_DV, OFF_RG // RET_DV
    return pl.pallas_call(
        _retention_kernel,
        out_shape=jax.ShapeDtypeStruct((batch * seq, RET_HEADS * RET_DV), BF16),
        grid=(RET_HEADS, batch),
        in_specs=[
            pl.BlockSpec((seq, RET_DK), lambda h, b: (b, qb + h)),
            pl.BlockSpec((seq, RET_DK), lambda h, b: (b, kb + h)),
            pl.BlockSpec((seq, RET_DV), lambda h, b: (b, vb + h)),
            pl.BlockSpec((seq, RET_DV), lambda h, b: (b, gb + h)),
            pl.BlockSpec((None, 1, RET_DV), lambda h, b: (layer, 0, h)),
            pl.BlockSpec((1, c, c), lambda h, b: (h, 0, 0)),
            pl.BlockSpec((1, c, RET_DV), lambda h, b: (h, 0, 0)),
            pl.BlockSpec((1, c, RET_DK), lambda h, b: (h, 0, 0)),
            pl.BlockSpec((1, 1, RET_DV), lambda h, b: (h, 0, 0)),
        ],
        out_specs=pl.BlockSpec((seq, RET_DV), lambda h, b: (b, h)),
        compiler_params=pltpu.CompilerParams(
            dimension_semantics=("parallel", "parallel"), vmem_limit_bytes=48 * 1024 * 1024),
        name="retention",
    )(proj, proj, proj, proj, gn_w3, dec, xi_b, zeta_b, gch_b)


def _bf16_terms(x, n):
    terms, rest = [], np.asarray(x, np.float64)
    for _ in range(n):
        t = rest.astype(np.float32).astype(BF16).astype(np.float64)
        terms.append(t)
        rest = rest - t
    return terms


def _moba_tables(seq):
    nb = seq // MOBA_BLOCK
    assert nb <= 8
    pos = np.arange(seq)
    kaug = np.zeros((seq, MOBA_DH), np.float32)
    kaug[pos, pos // MOBA_BLOCK] = 1.0
    kaug[:, 8:11] = (pos % MOBA_BLOCK)[:, None]
    kaug[:, 11:14] = (pos // MOBA_BLOCK * MOBA_BLOCK)[:, None]
    slopes = np.exp2(-8.0 * (np.arange(MOBA_HEADS, dtype=np.float64) + 1.0) / MOBA_HEADS)
    a_terms = _bf16_terms(slopes * np.log2(np.e), 3)
    eh = np.zeros((MOBA_HEADS, MOBA_AUG, MOBA_DH), np.float32)
    eh[:, np.arange(8), np.arange(8)] = 1.0
    for t, a in enumerate(a_terms):
        eh[:, 8, 8 + t] = a
        eh[:, 8, 11 + t] = a
    return jnp.asarray(kaug, dtype=BF16), jnp.asarray(eh, dtype=BF16)


def _moba_kernel(q_ref, k_ref, v_ref, g_ref, kaug_ref, eh_ref, o_ref,
                 s_ref, qa_ref, vt_ref, qt_ref, ka_ref):
    bs = MOBA_BLOCK
    dh = MOBA_DH
    seq = k_ref.shape[0]
    nb = seq // bs
    blk = lambda j: slice(j * bs, (j + 1) * bs)
    masked = -(2.0 ** 127)

    vt_ref[0:dh, :] = v_ref[...].T
    vt_ref[dh:dh + MOBA_AUG, :] = (lax.broadcasted_iota(jnp.int32, (MOBA_AUG, seq), 0) == 0).astype(BF16)
    qt_ref[...] = q_ref[...].T
    ka_ref[:, 0:dh] = k_ref[...]
    ka_ref[:, dh:2 * dh] = kaug_ref[...]

    km = jnp.concatenate(
        [jnp.sum(k_ref[blk(j), :].astype(F32), axis=0, keepdims=True) for j in range(nb)], axis=0) * (1.0 / bs)
    km_hi = km.astype(BF16)
    km_lo = (km - km_hi.astype(F32)).astype(BF16)
    km_hl = jnp.concatenate([km_hi, km_lo], axis=0)

    jidx = lax.broadcasted_iota(jnp.int32, (nb, bs), 0)
    kpos = lax.broadcasted_iota(jnp.int32, (bs, bs), 0)
    qpos = lax.broadcasted_iota(jnp.int32, (bs, bs), 1)
    causal = kpos <= qpos
    aug_tail = (lax.broadcasted_iota(jnp.int32, (MOBA_AUG - nb, bs), 0) == 0).astype(F32)
    eh = eh_ref[0]

    def query_side(mask_rows):
        aug = jnp.concatenate([mask_rows, aug_tail], axis=0).astype(BF16)
        return lax.dot_general(eh, aug, _TN, preferred_element_type=F32).astype(BF16)

    qt_right_all = query_side(jnp.zeros((nb, bs), F32))

    def query_block_setup(i):
        qt = qt_ref[:, blk(i)]
        if i > MOBA_TOPK:
            gate2 = jnp.dot(km_hl, qt, preferred_element_type=F32)
            gate = gate2[:nb] + gate2[nb:]
            rank = jnp.zeros((nb, bs), F32)
            for jj in range(i):
                row = gate[jj:jj + 1, :]
                rank = rank + ((row > gate) | ((row == gate) & (jj < jidx))).astype(F32)
            keep = (rank < float(MOBA_TOPK)) | (jidx >= i)
            qt_right = query_side(jnp.where(keep, 0.0, masked))
        else:
            qt_right = qt_right_all
        return jnp.concatenate([qt, qt_right], axis=0)

    for i in range(nb):
        qa_ref[i] = query_block_setup(i)

    def score_pass(i):
        qt_aug = qa_ref[i]
        mx = None
        for j in range(i + 1):
            s = jnp.dot(ka_ref[blk(j), :], qt_aug, preferred_element_type=F32)
            if j == i:
                s = jnp.where(causal, s, -jnp.inf)
            s_ref[i, blk(j), :] = s
            mj = jnp.max(s, axis=0, keepdims=True)
            mx = mj if mx is None else jnp.maximum(mx, mj)
        return mx

    def value_pass(i, mx):
        acc = None
        for j in range(i + 1):
            p = jnp.exp2(s_ref[i, blk(j), :] - mx).astype(BF16)
            pv = jnp.dot(vt_ref[:, blk(j)], p, preferred_element_type=F32)
            acc = pv if acc is None else acc + pv
        out = (acc[:dh] / acc[dh:dh + 1]).T
        g = g_ref[blk(i), :].astype(F32)
        o_ref[blk(i), :] = (out * (g * jax.nn.sigmoid(g))).astype(o_ref.dtype)

    order = list(reversed(range(nb)))
    maxes = {i: score_pass(i) for i in order}
    for i in order:
        value_pass(i, maxes[i])


def _moba(proj, tables, batch, seq):
    kaug, eh = tables
    bs = MOBA_BLOCK
    nb = seq // bs
    qb, kb, vb, gb = (OFF_MQ // MOBA_DH, OFF_MK // MOBA_DH, OFF_MV // MOBA_DH, OFF_MG // MOBA_DH)
    col = lambda base: pl.BlockSpec((seq, MOBA_DH), lambda h, b: (b, base + h))
    return pl.pallas_call(
        _moba_kernel,
        out_shape=jax.ShapeDtypeStruct((batch * seq, MOBA_HEADS * MOBA_DH), BF16),
        grid=(MOBA_HEADS, batch),
        in_specs=[col(qb), col(kb), col(vb), col(gb),
                  pl.BlockSpec((seq, MOBA_DH), lambda h, b: (0, 0)),
                  pl.BlockSpec((1, MOBA_AUG, MOBA_DH), lambda h, b: (h, 0, 0))],
        out_specs=col(0),
        scratch_shapes=[pltpu.VMEM((nb, seq, bs), F32),
                        pltpu.VMEM((nb, 2 * MOBA_DH, bs), BF16),
                        pltpu.VMEM((MOBA_DH + MOBA_AUG, seq), BF16),
                        pltpu.VMEM((MOBA_DH, seq), BF16),
                        pltpu.VMEM((seq, 2 * MOBA_DH), BF16)],
        compiler_params=pltpu.CompilerParams(
            dimension_semantics=("parallel", "parallel"), vmem_limit_bytes=48 * 1024 * 1024),
        name="moba",
    )(proj, proj, proj, proj, kaug, eh)


def _out_proj_kernel(ret_ref, mo_ref, gr_ref, gm_ref, x_ref, wr_ref, wm_ref, wo_ref, nw_ref,
                     *out_refs, last):
    r = jnp.dot(ret_ref[...], wr_ref[...], preferred_element_type=F32)
    m = jnp.dot(mo_ref[...], wm_ref[...], preferred_element_type=F32)
    y = jax.nn.sigmoid(gr_ref[...].astype(F32)) * r + jax.nn.sigmoid(gm_ref[...].astype(F32)) * m
    out = x_ref[...] + jnp.dot(y.astype(BF16), wo_ref[...], preferred_element_type=F32)
    ms = jnp.mean(out * out, axis=-1, keepdims=True)
    normed = out * lax.rsqrt(ms + NORM_EPS) * nw_ref[...]
    if last:
        out_refs[0][...] = normed
    else:
        out_refs[0][...] = out
        out_refs[1][...] = normed.astype(BF16)


def _out_proj(ret_g, mo_g, proj, x2, w_ret_o, w_moba_o, w_out, norm_w3, layer, norm_layer, last):
    m = x2.shape[0]
    d = D_MODEL
    tile = lambda col: pl.BlockSpec((OUT_TM, d), lambda i: (i, col))
    full = pl.BlockSpec((None, d, d), lambda i: (layer, 0, 0))
    x_shape = jax.ShapeDtypeStruct((m, d), F32)
    out_shape = x_shape if last else (x_shape, jax.ShapeDtypeStruct((m, d), BF16))
    out_specs = tile(0) if last else (tile(0), tile(0))
    return pl.pallas_call(
        functools.partial(_out_proj_kernel, last=last),
        out_shape=out_shape,
        grid=(m // OUT_TM,),
        in_specs=[tile(0), tile(0), tile(OFF_GR // d), tile(OFF_GM // d), tile(0),
                  full, full, full, pl.BlockSpec((None, 1, d), lambda i: (norm_layer, 0, 0))],
        out_specs=out_specs,
        compiler_params=pltpu.CompilerParams(
            dimension_semantics=("parallel",), vmem_limit_bytes=48 * 1024 * 1024),
        name="out_proj",
    )(ret_g, mo_g, proj, proj, x2, w_ret_o, w_moba_o, w_out, norm_w3)


def kernel(x, ln_w, w_in, ret_gn_w, w_ret_o, w_moba_o, w_out, final_norm_w):
    batch, seq, d = x.shape
    depth = w_in.shape[0]
    assert d == D_MODEL and w_in.shape[2] == D_IN
    assert seq % MOBA_BLOCK == 0 and seq % RET_CHUNK == 0 and (batch * seq) % IN_TM == 0
    ret_tables = _retention_tables()
    moba_tables = _moba_tables(seq)
    col_scale = _in_proj_col_scale()
    w_ret_b = w_ret_o.astype(BF16)
    w_moba_b = w_moba_o.astype(BF16)
    w_out_b = w_out.astype(BF16)
    ln_w3 = ln_w.reshape(depth, 1, d)
    gn_w3 = ret_gn_w.reshape(depth, 1, RET_HEADS * RET_DV)
    fn_w3 = final_norm_w.reshape(1, 1, d)
    x2 = x.reshape(batch * seq, d)
    h = _rmsnorm(x2, ln_w3, 0)
    for layer in range(depth):
        last = layer == depth - 1
        proj = _in_proj(h, w_in, col_scale, layer)
        ret_g = _retention(proj, gn_w3, layer, ret_tables, batch, seq)
        mo_g = _moba(proj, moba_tables, batch, seq)
        res = _out_proj(ret_g, mo_g, proj, x2, w_ret_b, w_moba_b, w_out_b,
                        fn_w3 if last else ln_w3, layer, 0 if last else layer + 1, last)
        if last:
            x2 = res
        else:
            x2, h = res
    return x2.reshape(batch, seq, d)
```

```python
import functools

import numpy as np
import jax
import jax.numpy as jnp
from jax import lax
from jax.experimental import pallas as pl
from jax.experimental.pallas import tpu as pltpu

D_MODEL = 1024
RET_HEADS = 4
RET_DK = 128
RET_DV = 256
RET_CHUNK = 256
MOBA_HEADS = 8
MOBA_DH = 128
MOBA_BLOCK = 256
MOBA_TOPK = 3
MOBA_EXP2_SCALE = MOBA_DH ** -0.5 * float(np.log2(np.e))
MOBA_AUG = 16
NORM_EPS = 1e-6

OFF_RQ, OFF_RK, OFF_RV, OFF_RG = 0, 512, 1024, 2048
OFF_MQ, OFF_MK, OFF_MV, OFF_MG = 3072, 4096, 5120, 6144
OFF_GR, OFF_GM = 7168, 8192
D_IN = 9216

NORM_TM = 512
IN_TM = 2048
IN_TN = 1536
OUT_TM = 512

F32 = jnp.float32
BF16 = jnp.bfloat16

_NT = (((1,), (1,)), ((), ()))
_TN = (((0,), (0,)), ((), ()))


def _rmsnorm_kernel(x_ref, w_ref, o_ref):
    x = x_ref[...]
    ms = jnp.mean(x * x, axis=-1, keepdims=True)
    o_ref[...] = (x * lax.rsqrt(ms + NORM_EPS) * w_ref[...]).astype(o_ref.dtype)


def _rmsnorm(x2, ln_w3, layer):
    m, d = x2.shape
    return pl.pallas_call(
        _rmsnorm_kernel,
        out_shape=jax.ShapeDtypeStruct((m, d), BF16),
        grid=(m // NORM_TM,),
        in_specs=[pl.BlockSpec((NORM_TM, d), lambda i: (i, 0)),
                  pl.BlockSpec((None, 1, d), lambda i: (layer, 0, 0))],
        out_specs=pl.BlockSpec((NORM_TM, d), lambda i: (i, 0)),
        compiler_params=pltpu.CompilerParams(dimension_semantics=("parallel",)),
        name="rmsnorm",
    )(x2, ln_w3)


def _in_proj_col_scale():
    cs = np.ones((1, D_IN), np.float32)
    cs[0, OFF_MQ:OFF_MQ + MOBA_HEADS * MOBA_DH] = MOBA_EXP2_SCALE
    return jnp.asarray(cs)


def _in_proj_kernel(h_ref, w_ref, cs_ref, o_ref, wb_ref):
    @pl.when(pl.program_id(1) == 0)
    def _():
        wb_ref[...] = (w_ref[...] * cs_ref[...]).astype(BF16)

    o_ref[...] = jnp.dot(h_ref[...], wb_ref[...], preferred_element_type=F32).astype(o_ref.dtype)


def _in_proj(h, w_in, col_scale, layer):
    m = h.shape[0]
    return pl.pallas_call(
        _in_proj_kernel,
        out_shape=jax.ShapeDtypeStruct((m, D_IN), BF16),
        grid=(D_IN // IN_TN, m // IN_TM),
        in_specs=[
            pl.BlockSpec((IN_TM, D_MODEL), lambda j, i: (i, 0)),
            pl.BlockSpec((None, D_MODEL, IN_TN), lambda j, i: (layer, 0, j)),
            pl.BlockSpec((1, IN_TN), lambda j, i: (0, j)),
        ],
        out_specs=pl.BlockSpec((IN_TM, IN_TN), lambda j, i: (i, j)),
        scratch_shapes=[pltpu.VMEM((D_MODEL, IN_TN), BF16)],
        compiler_params=pltpu.CompilerParams(
            dimension_semantics=("parallel", "arbitrary"),
            vmem_limit_bytes=48 * 1024 * 1024,
        ),
        name="in_proj",
    )(h, w_in, col_scale)


def _retention_tables():
    c = RET_CHUNK
    hh = np.arange(RET_HEADS, dtype=np.float64)
    log_g = np.log1p(-np.exp2(-5.0 - hh))
    n = np.arange(c, dtype=np.float64)
    rel = n[:, None] - n[None, :]
    scale = RET_DK ** -0.5
    dec = np.where(rel[None] >= 0, np.exp(np.maximum(rel, 0.0)[None] * log_g[:, None, None]), 0.0) * scale
    xi = np.exp((n + 1.0)[None, :] * log_g[:, None])
    zeta = np.exp((c - 1.0 - n)[None, :] * log_g[:, None]) * scale
    gch = np.exp(c * log_g)
    xi_b = np.broadcast_to(xi[:, :, None], (RET_HEADS, c, RET_DV))
    zeta_b = np.broadcast_to(zeta[:, :, None], (RET_HEADS, c, RET_DK))
    gch_b = np.broadcast_to(gch[:, None, None], (RET_HEADS, 1, RET_DV))
    f = lambda a: jnp.asarray(np.ascontiguousarray(a), dtype=F32)
    return f(dec), f(xi_b), f(zeta_b), f(gch_b)


def _retention_kernel(q_ref, k_ref, v_ref, g_ref, gnw_ref, dec_ref, xi_ref, zeta_ref, gch_ref, o_ref):
    c = RET_CHUNK
    n_chunks = q_ref.shape[0] // c
    st = None
    for ci in range(n_chunks):
        rows = slice(ci * c, (ci + 1) * c)
        qc = q_ref[rows, :]
        kc = k_ref[rows, :]
        vc = v_ref[rows, :]
        inner = lax.dot_general(qc, kc, _NT, preferred_element_type=F32) * dec_ref[0]
        o = jnp.dot(inner.astype(BF16), vc, preferred_element_type=F32)
        if st is not None:
            o = o + jnp.dot(qc, st.astype(BF16), preferred_element_type=F32) * xi_ref[0]
        if ci + 1 < n_chunks:
            kz = (kc.astype(F32) * zeta_ref[0]).astype(BF16)
            kv = lax.dot_general(kz, vc, _TN, preferred_element_type=F32)
            st = kv if st is None else st * gch_ref[0] + kv
        mu = jnp.mean(o, axis=-1, keepdims=True)
        d = o - mu
        var = jnp.mean(d * d, axis=-1, keepdims=True)
        y = d * lax.rsqrt(var + NORM_EPS) * gnw_ref[...]
        g = g_ref[rows, :].astype(F32)
        o_ref[rows, :] = (y * (g * jax.nn.sigmoid(g))).astype(o_ref.dtype)


def _retention(proj, gn_w3, layer, tables, batch, seq):
    dec, xi_b, zeta_b, gch_b = tables
    c = RET_CHUNK
    qb, kb = OFF_RQ // RET_DK, OFF_RK // RET_DK
    vb, gb = OFF_RV // RET_DV, OFF_RG // RET_DV
    return pl.pallas_call(
        _retention_kernel,
        out_shape=jax.ShapeDtypeStruct((batch * seq, RET_HEADS * RET_DV), BF16),
        grid=(RET_HEADS, batch),
        in_specs=[
            pl.BlockSpec((seq, RET_DK), lambda h, b: (b, qb + h)),
            pl.BlockSpec((seq, RET_DK), lambda h, b: (b, kb + h)),
            pl.BlockSpec((seq, RET_DV), lambda h, b: (b, vb + h)),
            pl.BlockSpec((seq, RET_DV), lambda h, b: (b, gb + h)),
            pl.BlockSpec((None, 1, RET_DV), lambda h, b: (layer, 0, h)),
            pl.BlockSpec((1, c, c), lambda h, b: (h, 0, 0)),
            pl.BlockSpec((1, c, RET_DV), lambda h, b: (h, 0, 0)),
            pl.BlockSpec((1, c, RET_DK), lambda h, b: (h, 0, 0)),
            pl.BlockSpec((1, 1, RET_DV), lambda h, b: (h, 0, 0)),
        ],
        out_specs=pl.BlockSpec((seq, RET_DV), lambda h, b: (b, h)),
        compiler_params=pltpu.CompilerParams(
            dimension_semantics=("parallel", "parallel"), vmem_limit_bytes=48 * 1024 * 1024),
        name="retention",
    )(proj, proj, proj, proj, gn_w3, dec, xi_b, zeta_b, gch_b)


def _bf16_terms(x, n):
    terms, rest = [], np.asarray(x, np.float64)
    for _ in range(n):
        t = rest.astype(np.float32).astype(BF16).astype(np.float64)
        terms.append(t)
        rest = rest - t
    return terms


def _moba_tables(seq):
    nb = seq // MOBA_BLOCK
    assert nb <= 8
    pos = np.arange(seq)
    kaug = np.zeros((seq, MOBA_DH), np.float32)
    kaug[pos, pos // MOBA_BLOCK] = 1.0
    kaug[:, 8:11] = (pos % MOBA_BLOCK)[:, None]
    kaug[:, 11:14] = (pos // MOBA_BLOCK * MOBA_BLOCK)[:, None]
    slopes = np.exp2(-8.0 * (np.arange(MOBA_HEADS, dtype=np.float64) + 1.0) / MOBA_HEADS)
    a_terms = _bf16_terms(slopes * np.log2(np.e), 3)
    eh = np.zeros((MOBA_HEADS, MOBA_AUG, MOBA_DH), np.float32)
    eh[:, np.arange(8), np.arange(8)] = 1.0
    for t, a in enumerate(a_terms):
        eh[:, 8, 8 + t] = a
        eh[:, 8, 11 + t] = a
    return jnp.asarray(kaug, dtype=BF16), jnp.asarray(eh, dtype=BF16)


def _moba_step(q_ref, k_ref, v_ref, g_ref, kaug_ref, eh_ref, o_ref, vt_ref,
               qa_w, qt_w, ka_w, qa_r, ka_r, s_w, mx_w, s_r, mx_r):
    bs = MOBA_BLOCK
    dh = MOBA_DH
    seq = kaug_ref.shape[0]
    nb = seq // bs
    blk = lambda j: slice(j * bs, (j + 1) * bs)
    pair = lambda i, j: slice((i * (i + 1) // 2 + j) * bs, (i * (i + 1) // 2 + j + 1) * bs)
    masked = -(2.0 ** 127)

    jidx = lax.broadcasted_iota(jnp.int32, (nb, bs), 0)
    kpos = lax.broadcasted_iota(jnp.int32, (bs, bs), 0)
    qpos = lax.broadcasted_iota(jnp.int32, (bs, bs), 1)
    causal = kpos <= qpos
    aug_tail = (lax.broadcasted_iota(jnp.int32, (MOBA_AUG - nb, bs), 0) == 0).astype(F32)
    eh = eh_ref[0]

    def query_side(mask_rows):
        aug = jnp.concatenate([mask_rows, aug_tail], axis=0).astype(BF16)
        return lax.dot_general(eh, aug, _TN, preferred_element_type=F32).astype(BF16)

    qt_w[...] = q_ref[...].T
    ka_w[:, 0:dh] = k_ref[...]
    ka_w[:, dh:2 * dh] = kaug_ref[...]
    km = jnp.concatenate(
        [jnp.sum(k_ref[blk(j), :].astype(F32), axis=0, keepdims=True) for j in range(nb)], axis=0) * (1.0 / bs)
    km_hi = km.astype(BF16)
    km_lo = (km - km_hi.astype(F32)).astype(BF16)
    km_hl = jnp.concatenate([km_hi, km_lo], axis=0)
    qt_right_all = query_side(jnp.zeros((nb, bs), F32))

    def setup_block(i):
        qt = qt_w[:, blk(i)]
        if i > MOBA_TOPK:
            gate2 = jnp.dot(km_hl, qt, preferred_element_type=F32)
            gate = gate2[:nb] + gate2[nb:]
            rank = jnp.zeros((nb, bs), F32)
            for jj in range(i):
                row = gate[jj:jj + 1, :]
                rank = rank + ((row > gate) | ((row == gate) & (jj < jidx))).astype(F32)
            keep = (rank < float(MOBA_TOPK)) | (jidx >= i)
            qt_right = query_side(jnp.where(keep, 0.0, masked))
        else:
            qt_right = qt_right_all
        qa_w[i] = jnp.concatenate([qt, qt_right], axis=0)

    vt_ref[0:dh, :] = v_ref[...].T
    vt_ref[dh:dh + MOBA_AUG, :] = (lax.broadcasted_iota(jnp.int32, (MOBA_AUG, seq), 0) == 0).astype(BF16)

    def score_pass(i):
        qt_aug = qa_r[i]
        mx = None
        for j in range(i + 1):
            s = jnp.dot(ka_r[blk(j), :], qt_aug, preferred_element_type=F32)
            if j == i:
                s = jnp.where(causal, s, -jnp.inf)
            s_w[pair(i, j), :] = s
            mj = jnp.max(s, axis=0, keepdims=True)
            mx = mj if mx is None else jnp.maximum(mx, mj)
        mx_w[i] = mx

    def value_pass(i):
        mx = mx_r[i]
        acc = None
        for j in range(i + 1):
            p = jnp.exp2(s_r[pair(i, j), :] - mx).astype(BF16)
            pv = jnp.dot(vt_ref[:, blk(j)], p, preferred_element_type=F32)
            acc = pv if acc is None else acc + pv
        out = (acc[:dh] / acc[dh:dh + 1]).T
        g = g_ref[blk(i), :].astype(F32)
        o_ref[blk(i), :] = (out * (g * jax.nn.sigmoid(g))).astype(o_ref.dtype)

    for i in reversed(range(nb)):
        score_pass(i)
        value_pass(i)
        setup_block(i)


def _moba_kernel(q_ref, k_ref, v_ref, g_ref, kaug_ref, eh_ref, o_ref, vt_ref,
                 qa0_ref, qt0_ref, ka0_ref, s0_ref, mx0_ref, qa1_ref, qt1_ref, ka1_ref, s1_ref, mx1_ref):
    t = pl.program_id(0)
    io = (q_ref, k_ref, v_ref, g_ref, kaug_ref, eh_ref, o_ref, vt_ref)

    @pl.when(t == 0)
    def _():
        for ref in (qa1_ref, ka1_ref, s0_ref, mx0_ref):
            ref[...] = jnp.zeros(ref.shape, ref.dtype)

    @pl.when(t % 2 == 0)
    def _():
        _moba_step(*io, qa0_ref, qt0_ref, ka0_ref, qa1_ref, ka1_ref, s1_ref, mx1_ref, s0_ref, mx0_ref)

    @pl.when(t % 2 == 1)
    def _():
        _moba_step(*io, qa1_ref, qt1_ref, ka1_ref, qa0_ref, ka0_ref, s0_ref, mx0_ref, s1_ref, mx1_ref)


def _moba(proj, tables, batch, seq):
    kaug, eh = tables
    bs = MOBA_BLOCK
    nb = seq // bs
    n = MOBA_HEADS * batch
    qb, kb, vb, gb = (OFF_MQ // MOBA_DH, OFF_MK // MOBA_DH, OFF_MV // MOBA_DH, OFF_MG // MOBA_DH)
    first = lambda t: jnp.minimum(t, n - 1)
    third = lambda t: jnp.clip(t - 2, 0, n - 1)
    col = lambda base, which: pl.BlockSpec(
        (seq, MOBA_DH), lambda t: (which(t) % batch, base + which(t) // batch))
    return pl.pallas_call(
        _moba_kernel,
        out_shape=jax.ShapeDtypeStruct((batch * seq, MOBA_HEADS * MOBA_DH), BF16),
        grid=(n + 2,),
        in_specs=[col(qb, first), col(kb, first), col(vb, third), col(gb, third),
                  pl.BlockSpec((seq, MOBA_DH), lambda t: (0, 0)),
                  pl.BlockSpec((1, MOBA_AUG, MOBA_DH), lambda t: (first(t) // batch, 0, 0))],
        out_specs=col(0, third),
        scratch_shapes=[pltpu.VMEM((MOBA_DH + MOBA_AUG, seq), BF16)]
        + 2 * [pltpu.VMEM((nb, 2 * MOBA_DH, bs), BF16),
               pltpu.VMEM((MOBA_DH, seq), BF16),
               pltpu.VMEM((seq, 2 * MOBA_DH), BF16),
               pltpu.VMEM((nb * (nb + 1) // 2 * bs, bs), F32),
               pltpu.VMEM((nb, 1, bs), F32)],
        compiler_params=pltpu.CompilerParams(
            dimension_semantics=("arbitrary",), vmem_limit_bytes=48 * 1024 * 1024),
        name="moba",
    )(proj, proj, proj, proj, kaug, eh)


def _out_proj_kernel(ret_ref, mo_ref, gr_ref, gm_ref, x_ref, wr_ref, wm_ref, wo_ref, nw_ref,
                     *out_refs, last):
    r = jnp.dot(ret_ref[...], wr_ref[...], preferred_element_type=F32)
    m = jnp.dot(mo_ref[...], wm_ref[...], preferred_element_type=F32)
    y = jax.nn.sigmoid(gr_ref[...].astype(F32)) * r + jax.nn.sigmoid(gm_ref[...].astype(F32)) * m
    out = x_ref[...] + jnp.dot(y.astype(BF16), wo_ref[...], preferred_element_type=F32)
    ms = jnp.mean(out * out, axis=-1, keepdims=True)
    normed = out * lax.rsqrt(ms + NORM_EPS) * nw_ref[...]
    if last:
        out_refs[0][...] = normed
    else:
        out_refs[0][...] = out
        out_refs[1][...] = normed.astype(BF16)


def _out_proj(ret_g, mo_g, proj, x2, w_ret_o, w_moba_o, w_out, norm_w3, layer, norm_layer, last):
    m = x2.shape[0]
    d = D_MODEL
    tile = lambda col: pl.BlockSpec((OUT_TM, d), lambda i: (i, col))
    full = pl.BlockSpec((None, d, d), lambda i: (layer, 0, 0))
    x_shape = jax.ShapeDtypeStruct((m, d), F32)
    out_shape = x_shape if last else (x_shape, jax.ShapeDtypeStruct((m, d), BF16))
    out_specs = tile(0) if last else (tile(0), tile(0))
    return pl.pallas_call(
        functools.partial(_out_proj_kernel, last=last),
        out_shape=out_shape,
        grid=(m // OUT_TM,),
        in_specs=[tile(0), tile(0), tile(OFF_GR // d), tile(OFF_GM // d), tile(0),
                  full, full, full, pl.BlockSpec((None, 1, d), lambda i: (norm_layer, 0, 0))],
        out_specs=out_specs,
        compiler_params=pltpu.CompilerParams(
            dimension_semantics=("parallel",), vmem_limit_bytes=48 * 1024 * 1024),
        name="out_proj",
    )(ret_g, mo_g, proj, proj, x2, w_ret_o, w_moba_o, w_out, norm_w3)


def kernel(x, ln_w, w_in, ret_gn_w, w_ret_o, w_moba_o, w_out, final_norm_w):
    batch, seq, d = x.shape
    depth = w_in.shape[0]
    assert d == D_MODEL and w_in.shape[2] == D_IN
    assert seq % MOBA_BLOCK == 0 and seq % RET_CHUNK == 0 and (batch * seq) % IN_TM == 0
    ret_tables = _retention_tables()
    moba_tables = _moba_tables(seq)
    col_scale = _in_proj_col_scale()
    w_ret_b = w_ret_o.astype(BF16)
    w_moba_b = w_moba_o.astype(BF16)
    w_out_b = w_out.astype(BF16)
    ln_w3 = ln_w.reshape(depth, 1, d)
    gn_w3 = ret_gn_w.reshape(depth, 1, RET_HEADS * RET_DV)
    fn_w3 = final_norm_w.reshape(1, 1, d)
    x2 = x.reshape(batch * seq, d)
    h = _rmsnorm(x2, ln_w3, 0)
    for layer in range(depth):
        last = layer == depth - 1
        proj = _in_proj(h, w_in, col_scale, layer)
        ret_g = _retention(proj, gn_w3, layer, ret_tables, batch, seq)
        mo_g = _moba(proj, moba_tables, batch, seq)
        res = _out_proj(ret_g, mo_g, proj, x2, w_ret_b, w_moba_b, w_out_b,
                        fn_w3 if last else ln_w3, layer, 0 if last else layer + 1, last)
        if last:
            x2 = res
        else:
            x2, h = res
    return x2.reshape(batch, seq, d)
```

```python
import functools

import numpy as np
import jax
import jax.numpy as jnp
from jax import lax
from jax.experimental import pallas as pl
from jax.experimental.pallas import tpu as pltpu

D_MODEL = 1024
RET_HEADS = 4
RET_DK = 128
RET_DV = 256
RET_CHUNK = 256
RET_SEQS_PER_STEP = 2
MOBA_HEADS = 8
MOBA_DH = 128
MOBA_BLOCK = 256
MOBA_TOPK = 3
MOBA_EXP2_SCALE = MOBA_DH ** -0.5 * float(np.log2(np.e))
MOBA_AUG = 16
NORM_EPS = 1e-6

OFF_RQ, OFF_RK, OFF_RV, OFF_RG = 0, 512, 1024, 2048
OFF_MQ, OFF_MK, OFF_MV, OFF_MG = 3072, 4096, 5120, 6144
OFF_GR, OFF_GM = 7168, 8192
D_IN = 9216

NORM_TM = 2048
IN_TM = 2048
IN_TN = 1536
OUT_TM = 1024

V7X_VMEM_BYTES = 64 * 2 ** 20
VMEM_LIMIT = V7X_VMEM_BYTES * 3 // 4
VMEM_LIMIT_OUT_PROJ = V7X_VMEM_BYTES * 29 // 32

F32 = jnp.float32
BF16 = jnp.bfloat16

_NT = (((1,), (1,)), ((), ()))
_TN = (((0,), (0,)), ((), ()))


def _rmsnorm_kernel(x_ref, w_ref, o_ref):
    x = x_ref[...]
    ms = jnp.mean(x * x, axis=-1, keepdims=True)
    o_ref[...] = (x * lax.rsqrt(ms + NORM_EPS) * w_ref[...]).astype(o_ref.dtype)


def _rmsnorm(x2, ln_w3, layer):
    m, d = x2.shape
    return pl.pallas_call(
        _rmsnorm_kernel,
        out_shape=jax.ShapeDtypeStruct((m, d), BF16),
        grid=(m // NORM_TM,),
        in_specs=[pl.BlockSpec((NORM_TM, d), lambda i: (i, 0)),
                  pl.BlockSpec((None, 1, d), lambda i: (layer, 0, 0))],
        out_specs=pl.BlockSpec((NORM_TM, d), lambda i: (i, 0)),
        compiler_params=pltpu.CompilerParams(
            dimension_semantics=("parallel",), vmem_limit_bytes=VMEM_LIMIT),
        name="rmsnorm",
    )(x2, ln_w3)


def _in_proj_col_scale():
    cs = np.ones((1, D_IN), np.float32)
    cs[0, OFF_MQ:OFF_MQ + MOBA_HEADS * MOBA_DH] = MOBA_EXP2_SCALE
    return jnp.asarray(cs)


def _in_proj_kernel(h_ref, w_ref, cs_ref, o_ref, wb_ref):
    @pl.when(pl.program_id(1) == 0)
    def _():
        wb_ref[...] = (w_ref[...] * cs_ref[...]).astype(BF16)

    o_ref[...] = jnp.dot(h_ref[...], wb_ref[...], preferred_element_type=F32).astype(o_ref.dtype)


def _in_proj(h, w_in, col_scale, layer):
    m = h.shape[0]
    return pl.pallas_call(
        _in_proj_kernel,
        out_shape=jax.ShapeDtypeStruct((m, D_IN), BF16),
        grid=(D_IN // IN_TN, m // IN_TM),
        in_specs=[
            pl.BlockSpec((IN_TM, D_MODEL), lambda j, i: (i, 0)),
            pl.BlockSpec((None, D_MODEL, IN_TN), lambda j, i: (layer, 0, j)),
            pl.BlockSpec((1, IN_TN), lambda j, i: (0, j)),
        ],
        out_specs=pl.BlockSpec((IN_TM, IN_TN), lambda j, i: (i, j)),
        scratch_shapes=[pltpu.VMEM((D_MODEL, IN_TN), BF16)],
        compiler_params=pltpu.CompilerParams(
            dimension_semantics=("parallel", "arbitrary"),
            vmem_limit_bytes=VMEM_LIMIT,
        ),
        name="in_proj",
    )(h, w_in, col_scale)


def _retention_tables():
    c = RET_CHUNK
    hh = np.arange(RET_HEADS, dtype=np.float64)
    log_g = np.log1p(-np.exp2(-5.0 - hh))
    n = np.arange(c, dtype=np.float64)
    rel = n[:, None] - n[None, :]
    scale = RET_DK ** -0.5
    dec = np.where(rel[None] >= 0, np.exp(np.maximum(rel, 0.0)[None] * log_g[:, None, None]), 0.0) * scale
    xi = np.exp((n + 1.0)[None, :] * log_g[:, None])
    zeta = np.exp((c - 1.0 - n)[None, :] * log_g[:, None]) * scale
    gch = np.exp(c * log_g)
    xi_b = np.broadcast_to(xi[:, :, None], (RET_HEADS, c, RET_DV))
    zeta_b = np.broadcast_to(zeta[:, :, None], (RET_HEADS, c, RET_DK))
    gch_b = np.broadcast_to(gch[:, None, None], (RET_HEADS, 1, RET_DV))
    f = lambda a: jnp.asarray(np.ascontiguousarray(a), dtype=F32)
    return f(dec), f(xi_b), f(zeta_b), f(gch_b)


def _retention_kernel(q_ref, k_ref, v_ref, g_ref, gnw_ref, dec_ref, xi_ref, zeta_ref, gch_ref, o_ref):
    c = RET_CHUNK
    seq = q_ref.shape[0] // RET_SEQS_PER_STEP
    n_chunks = seq // c
    states = [None] * RET_SEQS_PER_STEP
    for ci in range(n_chunks):
        for u in range(RET_SEQS_PER_STEP):
            rows = slice(u * seq + ci * c, u * seq + (ci + 1) * c)
            st = states[u]
            qc = q_ref[rows, :]
            kc = k_ref[rows, :]
            vc = v_ref[rows, :]
            inner = lax.dot_general(qc, kc, _NT, preferred_element_type=F32) * dec_ref[0]
            o = jnp.dot(inner.astype(BF16), vc, preferred_element_type=F32)
            if st is not None:
                o = o + jnp.dot(qc, st.astype(BF16), preferred_element_type=F32) * xi_ref[0]
            if ci + 1 < n_chunks:
                kz = (kc.astype(F32) * zeta_ref[0]).astype(BF16)
                kv = lax.dot_general(kz, vc, _TN, preferred_element_type=F32)
                states[u] = kv if st is None else st * gch_ref[0] + kv
            mu = jnp.mean(o, axis=-1, keepdims=True)
            d = o - mu
            var = jnp.mean(d * d, axis=-1, keepdims=True)
            y = d * lax.rsqrt(var + NORM_EPS) * gnw_ref[...]
            g = g_ref[rows, :].astype(F32)
            o_ref[rows, :] = (y * (g * jax.nn.sigmoid(g))).astype(o_ref.dtype)


def _retention(proj, gn_w3, layer, tables, batch, seq):
    dec, xi_b, zeta_b, gch_b = tables
    c = RET_CHUNK
    qb, kb = OFF_RQ // RET_DK, OFF_RK // RET_DK
    vb, gb = OFF_RV // RET_DV, OFF_RG // RET_DV
    n_sub = RET_SEQS_PER_STEP
    assert batch % n_sub == 0
    rows = n_sub * seq
    return pl.pallas_call(
        _retention_kernel,
        out_shape=jax.ShapeDtypeStruct((batch * seq, RET_HEADS * RET_DV), BF16),
        grid=(RET_HEADS, batch // n_sub),
        in_specs=[
            pl.BlockSpec((rows, RET_DK), lambda h, b: (b, qb + h)),
            pl.BlockSpec((rows, RET_DK), lambda h, b: (b, kb + h)),
            pl.BlockSpec((rows, RET_DV), lambda h, b: (b, vb + h)),
            pl.BlockSpec((rows, RET_DV), lambda h, b: (b, gb + h)),
            pl.BlockSpec((None, 1, RET_DV), lambda h, b: (layer, 0, h)),
            pl.BlockSpec((1, c, c), lambda h, b: (h, 0, 0)),
            pl.BlockSpec((1, c, RET_DV), lambda h, b: (h, 0, 0)),
            pl.BlockSpec((1, c, RET_DK), lambda h, b: (h, 0, 0)),
            pl.BlockSpec((1, 1, RET_DV), lambda h, b: (h, 0, 0)),
        ],
        out_specs=pl.BlockSpec((rows, RET_DV), lambda h, b: (b, h)),
        compiler_params=pltpu.CompilerParams(
            dimension_semantics=("parallel", "parallel"), vmem_limit_bytes=VMEM_LIMIT),
        name="retention",
    )(proj, proj, proj, proj, gn_w3, dec, xi_b, zeta_b, gch_b)


def _bf16_terms(x, n):
    terms, rest = [], np.asarray(x, np.float64)
    for _ in range(n):
        t = rest.astype(np.float32).astype(BF16).astype(np.float64)
        terms.append(t)
        rest = rest - t
    return terms


def _moba_tables(seq):
    nb = seq // MOBA_BLOCK
    assert nb <= 8
    pos = np.arange(seq)
    kaug = np.zeros((seq, MOBA_DH), np.float32)
    kaug[pos, pos // MOBA_BLOCK] = 1.0
    kaug[:, 8:11] = (pos % MOBA_BLOCK)[:, None]
    kaug[:, 11:14] = (pos // MOBA_BLOCK * MOBA_BLOCK)[:, None]
    slopes = np.exp2(-8.0 * (np.arange(MOBA_HEADS, dtype=np.float64) + 1.0) / MOBA_HEADS)
    a_terms = _bf16_terms(slopes * np.log2(np.e), 3)
    eh = np.zeros((MOBA_HEADS, MOBA_AUG, MOBA_DH), np.float32)
    eh[:, np.arange(8), np.arange(8)] = 1.0
    for t, a in enumerate(a_terms):
        eh[:, 8, 8 + t] = a
        eh[:, 8, 11 + t] = a
    return jnp.asarray(kaug, dtype=BF16), jnp.asarray(eh, dtype=BF16)


def _moba_step(q_ref, k_ref, v_ref, g_ref, kaug_ref, eh_ref, o_ref, vt_ref,
               qa_w, qt_w, ka_w, qa_r, ka_r, s_w, mx_w, s_r, mx_r):
    bs = MOBA_BLOCK
    dh = MOBA_DH
    seq = kaug_ref.shape[0]
    nb = seq // bs
    blk = lambda j: slice(j * bs, (j + 1) * bs)
    pair = lambda i, j: slice((i * (i + 1) // 2 + j) * bs, (i * (i + 1) // 2 + j + 1) * bs)
    masked = -(2.0 ** 127)

    jidx = lax.broadcasted_iota(jnp.int32, (nb, bs), 0)
    kpos = lax.broadcasted_iota(jnp.int32, (bs, bs), 0)
    qpos = lax.broadcasted_iota(jnp.int32, (bs, bs), 1)
    causal = kpos <= qpos
    aug_tail = (lax.broadcasted_iota(jnp.int32, (MOBA_AUG - nb, bs), 0) == 0).astype(F32)
    eh = eh_ref[0]

    def query_side(mask_rows):
        aug = jnp.concatenate([mask_rows, aug_tail], axis=0).astype(BF16)
        return lax.dot_general(eh, aug, _TN, preferred_element_type=F32).astype(BF16)

    qt_w[...] = q_ref[...].T
    ka_w[:, 0:dh] = k_ref[...]
    ka_w[:, dh:2 * dh] = kaug_ref[...]
    km = jnp.concatenate(
        [jnp.sum(k_ref[blk(j), :].astype(F32), axis=0, keepdims=True) for j in range(nb)], axis=0) * (1.0 / bs)
    km_hi = km.astype(BF16)
    km_lo = (km - km_hi.astype(F32)).astype(BF16)
    km_hl = jnp.concatenate([km_hi, km_lo], axis=0)
    qt_right_all = query_side(jnp.zeros((nb, bs), F32))

    def setup_block(i):
        qt = qt_w[:, blk(i)]
        if i > MOBA_TOPK:
            gate2 = jnp.dot(km_hl, qt, preferred_element_type=F32)
            gate = gate2[:nb] + gate2[nb:]
            rank = jnp.zeros((nb, bs), F32)
            for jj in range(i):
                row = gate[jj:jj + 1, :]
                rank = rank + ((row > gate) | ((row == gate) & (jj < jidx))).astype(F32)
            keep = (rank < float(MOBA_TOPK)) | (jidx >= i)
            qt_right = query_side(jnp.where(keep, 0.0, masked))
        else:
            qt_right = qt_right_all
        qa_w[i] = jnp.concatenate([qt, qt_right], axis=0)

    vt_ref[0:dh, :] = v_ref[...].T
    vt_ref[dh:dh + MOBA_AUG, :] = (lax.broadcasted_iota(jnp.int32, (MOBA_AUG, seq), 0) == 0).astype(BF16)

    def score_pass(i):
        qt_aug = qa_r[i]
        mx = None
        for j in range(i + 1):
            s = jnp.dot(ka_r[blk(j), :], qt_aug, preferred_element_type=F32)
            if j == i:
                s = jnp.where(causal, s, -jnp.inf)
            s_w[pair(i, j), :] = s
            mj = jnp.max(s, axis=0, keepdims=True)
            mx = mj if mx is None else jnp.maximum(mx, mj)
        mx_w[i] = mx

    def value_pass(i):
        mx = mx_r[i]
        acc = None
        for j in range(i + 1):
            p = jnp.exp2(s_r[pair(i, j), :] - mx).astype(BF16)
            pv = jnp.dot(vt_ref[:, blk(j)], p, preferred_element_type=F32)
            acc = pv if acc is None else acc + pv
        out = (acc[:dh] / acc[dh:dh + 1]).T
        g = g_ref[blk(i), :].astype(F32)
        o_ref[blk(i), :] = (out * (g * jax.nn.sigmoid(g))).astype(o_ref.dtype)

    for i in reversed(range(nb)):
        score_pass(i)
        value_pass(i)
        setup_block(i)


def _moba_kernel(q_ref, k_ref, v_ref, g_ref, kaug_ref, eh_ref, o_ref, vt_ref,
                 qa0_ref, qt0_ref, ka0_ref, s0_ref, mx0_ref, qa1_ref, qt1_ref, ka1_ref, s1_ref, mx1_ref):
    t = pl.program_id(0)
    io = (q_ref, k_ref, v_ref, g_ref, kaug_ref, eh_ref, o_ref, vt_ref)

    @pl.when(t == 0)
    def _():
        for ref in (qa1_ref, ka1_ref, s0_ref, mx0_ref):
            ref[...] = jnp.zeros(ref.shape, ref.dtype)

    @pl.when(t % 2 == 0)
    def _():
        _moba_step(*io, qa0_ref, qt0_ref, ka0_ref, qa1_ref, ka1_ref, s1_ref, mx1_ref, s0_ref, mx0_ref)

    @pl.when(t % 2 == 1)
    def _():
        _moba_step(*io, qa1_ref, qt1_ref, ka1_ref, qa0_ref, ka0_ref, s0_ref, mx0_ref, s1_ref, mx1_ref)


def _moba(proj, tables, batch, seq):
    kaug, eh = tables
    bs = MOBA_BLOCK
    nb = seq // bs
    n = MOBA_HEADS * batch
    qb, kb, vb, gb = (OFF_MQ // MOBA_DH, OFF_MK // MOBA_DH, OFF_MV // MOBA_DH, OFF_MG // MOBA_DH)
    first = lambda t: jnp.minimum(t, n - 1)
    third = lambda t: jnp.clip(t - 2, 0, n - 1)
    col = lambda base, which: pl.BlockSpec(
        (seq, MOBA_DH), lambda t: (which(t) % batch, base + which(t) // batch))
    return pl.pallas_call(
        _moba_kernel,
        out_shape=jax.ShapeDtypeStruct((batch * seq, MOBA_HEADS * MOBA_DH), BF16),
        grid=(n + 2,),
        in_specs=[col(qb, first), col(kb, first), col(vb, third), col(gb, third),
                  pl.BlockSpec((seq, MOBA_DH), lambda t: (0, 0)),
                  pl.BlockSpec((1, MOBA_AUG, MOBA_DH), lambda t: (first(t) // batch, 0, 0))],
        out_specs=col(0, third),
        scratch_shapes=[pltpu.VMEM((MOBA_DH + MOBA_AUG, seq), BF16)]
        + 2 * [pltpu.VMEM((nb, 2 * MOBA_DH, bs), BF16),
               pltpu.VMEM((MOBA_DH, seq), BF16),
               pltpu.VMEM((seq, 2 * MOBA_DH), BF16),
               pltpu.VMEM((nb * (nb + 1) // 2 * bs, bs), F32),
               pltpu.VMEM((nb, 1, bs), F32)],
        compiler_params=pltpu.CompilerParams(
            dimension_semantics=("arbitrary",), vmem_limit_bytes=VMEM_LIMIT),
        name="moba",
    )(proj, proj, proj, proj, kaug, eh)


def _out_proj_kernel(ret_ref, mo_ref, gr_ref, gm_ref, x_ref, wr_ref, wm_ref, wo_ref, nw_ref,
                     *out_refs, last):
    r = jnp.dot(ret_ref[...], wr_ref[...], preferred_element_type=F32)
    m = jnp.dot(mo_ref[...], wm_ref[...], preferred_element_type=F32)
    y = jax.nn.sigmoid(gr_ref[...].astype(F32)) * r + jax.nn.sigmoid(gm_ref[...].astype(F32)) * m
    out = x_ref[...] + jnp.dot(y.astype(BF16), wo_ref[...], preferred_element_type=F32)
    ms = jnp.mean(out * out, axis=-1, keepdims=True)
    normed = out * lax.rsqrt(ms + NORM_EPS) * nw_ref[...]
    if last:
        out_refs[0][...] = normed
    else:
        out_refs[0][...] = out
        out_refs[1][...] = normed.astype(BF16)


def _out_proj(ret_g, mo_g, proj, x2, w_ret_o, w_moba_o, w_out, norm_w3, layer, norm_layer, last):
    m = x2.shape[0]
    d = D_MODEL
    tile = lambda col: pl.BlockSpec((OUT_TM, d), lambda i: (i, col))
    full = pl.BlockSpec((None, d, d), lambda i: (layer, 0, 0))
    x_shape = jax.ShapeDtypeStruct((m, d), F32)
    out_shape = x_shape if last else (x_shape, jax.ShapeDtypeStruct((m, d), BF16))
    out_specs = tile(0) if last else (tile(0), tile(0))
    return pl.pallas_call(
        functools.partial(_out_proj_kernel, last=last),
        out_shape=out_shape,
        grid=(m // OUT_TM,),
        in_specs=[tile(0), tile(0), tile(OFF_GR // d), tile(OFF_GM // d), tile(0),
                  full, full, full, pl.BlockSpec((None, 1, d), lambda i: (norm_layer, 0, 0))],
        out_specs=out_specs,
        compiler_params=pltpu.CompilerParams(
            dimension_semantics=("parallel",), vmem_limit_bytes=VMEM_LIMIT_OUT_PROJ),
        name="out_proj",
    )(ret_g, mo_g, proj, proj, x2, w_ret_o, w_moba_o, w_out, norm_w3)


def kernel(x, ln_w, w_in, ret_gn_w, w_ret_o, w_moba_o, w_out, final_norm_w):
    batch, seq, d = x.shape
    depth = w_in.shape[0]
    assert d == D_MODEL and w_in.shape[2] == D_IN
    assert seq % MOBA_BLOCK == 0 and seq % RET_CHUNK == 0 and (batch * seq) % IN_TM == 0
    ret_tables = _retention_tables()
    moba_tables = _moba_tables(seq)
    col_scale = _in_proj_col_scale()
    w_ret_b = w_ret_o.astype(BF16)
    w_moba_b = w_moba_o.astype(BF16)
    w_out_b = w_out.astype(BF16)
    ln_w3 = ln_w.reshape(depth, 1, d)
    gn_w3 = ret_gn_w.reshape(depth, 1, RET_HEADS * RET_DV)
    fn_w3 = final_norm_w.reshape(1, 1, d)
    x2 = x.reshape(batch * seq, d)
    h = _rmsnorm(x2, ln_w3, 0)
    for layer in range(depth):
        last = layer == depth - 1
        proj = _in_proj(h, w_in, col_scale, layer)
        ret_g = _retention(proj, gn_w3, layer, ret_tables, batch, seq)
        mo_g = _moba(proj, moba_tables, batch, seq)
        res = _out_proj(ret_g, mo_g, proj, x2, w_ret_b, w_moba_b, w_out_b,
                        fn_w3 if last else ln_w3, layer, 0 if last else layer + 1, last)
        if last:
            x2 = res
        else:
            x2, h = res
    return x2.reshape(batch, seq, d)
```

```python
import functools

import numpy as np
import jax
import jax.numpy as jnp
from jax import lax
from jax.experimental import pallas as pl
from jax.experimental.pallas import tpu as pltpu

D_MODEL = 1024
RET_HEADS = 4
RET_DK = 128
RET_DV = 256
RET_CHUNK = 256
RET_SEQS_PER_STEP = 4
MOBA_HEADS = 8
MOBA_DH = 128
MOBA_BLOCK = 256
MOBA_TOPK = 3
MOBA_EXP2_SCALE = MOBA_DH ** -0.5 * float(np.log2(np.e))
MOBA_AUG = 16
NORM_EPS = 1e-6

OFF_RQ, OFF_RK, OFF_RV, OFF_RG = 0, 512, 1024, 2048
OFF_MQ, OFF_MK, OFF_MV, OFF_MG = 3072, 4096, 5120, 6144
OFF_GR, OFF_GM = 7168, 8192
D_IN = 9216

NORM_TM = 2048
IN_TM = 2048
IN_TN = 1536
OUT_TM = 1024

V7X_VMEM_BYTES = 64 * 2 ** 20
VMEM_LIMIT = V7X_VMEM_BYTES * 3 // 4
VMEM_LIMIT_OUT_PROJ = V7X_VMEM_BYTES * 29 // 32

F32 = jnp.float32
BF16 = jnp.bfloat16

_NT = (((1,), (1,)), ((), ()))
_TN = (((0,), (0,)), ((), ()))


def _rmsnorm_kernel(x_ref, w_ref, o_ref):
    x = x_ref[...]
    ms = jnp.mean(x * x, axis=-1, keepdims=True)
    o_ref[...] = (x * lax.rsqrt(ms + NORM_EPS) * w_ref[...]).astype(o_ref.dtype)


def _rmsnorm(x2, ln_w3, layer):
    m, d = x2.shape
    return pl.pallas_call(
        _rmsnorm_kernel,
        out_shape=jax.ShapeDtypeStruct((m, d), BF16),
        grid=(m // NORM_TM,),
        in_specs=[pl.BlockSpec((NORM_TM, d), lambda i: (i, 0)),
                  pl.BlockSpec((None, 1, d), lambda i: (layer, 0, 0))],
        out_specs=pl.BlockSpec((NORM_TM, d), lambda i: (i, 0)),
        compiler_params=pltpu.CompilerParams(
            dimension_semantics=("parallel",), vmem_limit_bytes=VMEM_LIMIT),
        name="rmsnorm",
    )(x2, ln_w3)


def _in_proj_col_scale():
    cs = np.ones((1, D_IN), np.float32)
    cs[0, OFF_MQ:OFF_MQ + MOBA_HEADS * MOBA_DH] = MOBA_EXP2_SCALE
    return jnp.asarray(cs)


def _in_proj_kernel(h_ref, w_ref, cs_ref, o_ref, wb_ref):
    @pl.when(pl.program_id(1) == 0)
    def _():
        wb_ref[...] = (w_ref[...] * cs_ref[...]).astype(BF16)

    o_ref[...] = jnp.dot(h_ref[...], wb_ref[...], preferred_element_type=F32).astype(o_ref.dtype)


def _in_proj(h, w_in, col_scale, layer):
    m = h.shape[0]
    return pl.pallas_call(
        _in_proj_kernel,
        out_shape=jax.ShapeDtypeStruct((m, D_IN), BF16),
        grid=(D_IN // IN_TN, m // IN_TM),
        in_specs=[
            pl.BlockSpec((IN_TM, D_MODEL), lambda j, i: (i, 0)),
            pl.BlockSpec((None, D_MODEL, IN_TN), lambda j, i: (layer, 0, j)),
            pl.BlockSpec((1, IN_TN), lambda j, i: (0, j)),
        ],
        out_specs=pl.BlockSpec((IN_TM, IN_TN), lambda j, i: (i, j)),
        scratch_shapes=[pltpu.VMEM((D_MODEL, IN_TN), BF16)],
        compiler_params=pltpu.CompilerParams(
            dimension_semantics=("parallel", "arbitrary"),
            vmem_limit_bytes=VMEM_LIMIT,
        ),
        name="in_proj",
    )(h, w_in, col_scale)


def _retention_tables():
    c = RET_CHUNK
    hh = np.arange(RET_HEADS, dtype=np.float64)
    log_g = np.log1p(-np.exp2(-5.0 - hh))
    n = np.arange(c, dtype=np.float64)
    rel = n[:, None] - n[None, :]
    scale = RET_DK ** -0.5
    dec = np.where(rel[None] >= 0, np.exp(np.maximum(rel, 0.0)[None] * log_g[:, None, None]), 0.0) * scale
    xi = np.exp((n + 1.0)[None, :] * log_g[:, None])
    zeta = np.exp((c - 1.0 - n)[None, :] * log_g[:, None]) * scale
    gch = np.exp(c * log_g)
    xi_b = np.broadcast_to(xi[:, :, None], (RET_HEADS, c, RET_DV))
    zeta_b = np.broadcast_to(zeta[:, :, None], (RET_HEADS, c, RET_DK))
    gch_b = np.broadcast_to(gch[:, None, None], (RET_HEADS, 1, RET_DV))
    f = lambda a: jnp.asarray(np.ascontiguousarray(a), dtype=F32)
    return f(dec), f(xi_b), f(zeta_b), f(gch_b)


def _retention_kernel(q_ref, k_ref, v_ref, g_ref, gnw_ref, dec_ref, xi_ref, zeta_ref, gch_ref, o_ref):
    c = RET_CHUNK
    seq = q_ref.shape[0] // RET_SEQS_PER_STEP
    n_chunks = seq // c
    states = [None] * RET_SEQS_PER_STEP
    for ci in range(n_chunks):
        for u in range(RET_SEQS_PER_STEP):
            rows = slice(u * seq + ci * c, u * seq + (ci + 1) * c)
            st = states[u]
            qc = q_ref[rows, :]
            kc = k_ref[rows, :]
            vc = v_ref[rows, :]
            inner = lax.dot_general(qc, kc, _NT, preferred_element_type=F32) * dec_ref[0]
            o = jnp.dot(inner.astype(BF16), vc, preferred_element_type=F32)
            if st is not None:
                o = o + jnp.dot(qc, st.astype(BF16), preferred_element_type=F32) * xi_ref[0]
            if ci + 1 < n_chunks:
                kz = (kc.astype(F32) * zeta_ref[0]).astype(BF16)
                kv = lax.dot_general(kz, vc, _TN, preferred_element_type=F32)
                states[u] = kv if st is None else st * gch_ref[0] + kv
            mu = jnp.mean(o, axis=-1, keepdims=True)
            d = o - mu
            var = jnp.mean(d * d, axis=-1, keepdims=True)
            y = d * lax.rsqrt(var + NORM_EPS) * gnw_ref[...]
            g = g_ref[rows, :].astype(F32)
            o_ref[rows, :] = (y * (g * jax.nn.sigmoid(g))).astype(o_ref.dtype)


def _retention(proj, gn_w3, layer, tables, batch, seq):
    dec, xi_b, zeta_b, gch_b = tables
    c = RET_CHUNK
    qb, kb = OFF_RQ // RET_DK, OFF_RK // RET_DK
    vb, gb = OFF_RV // RET_DV, OFF_RG // RET_DV
    n_sub = RET_SEQS_PER_STEP
    assert batch % n_sub == 0
    rows = n_sub * seq
    return pl.pallas_call(
        _retention_kernel,
        out_shape=jax.ShapeDtypeStruct((batch * seq, RET_HEADS * RET_DV), BF16),
        grid=(RET_HEADS, batch // n_sub),
        in_specs=[
            pl.BlockSpec((rows, RET_DK), lambda h, b: (b, qb + h)),
            pl.BlockSpec((rows, RET_DK), lambda h, b: (b, kb + h)),
            pl.BlockSpec((rows, RET_DV), lambda h, b: (b, vb + h)),
            pl.BlockSpec((rows, RET_DV), lambda h, b: (b, gb + h)),
            pl.BlockSpec((None, 1, RET_DV), lambda h, b: (layer, 0, h)),
            pl.BlockSpec((1, c, c), lambda h, b: (h, 0, 0)),
            pl.BlockSpec((1, c, RET_DV), lambda h, b: (h, 0, 0)),
            pl.BlockSpec((1, c, RET_DK), lambda h, b: (h, 0, 0)),
            pl.BlockSpec((1, 1, RET_DV), lambda h, b: (h, 0, 0)),
        ],
        out_specs=pl.BlockSpec((rows, RET_DV), lambda h, b: (b, h)),
        compiler_params=pltpu.CompilerParams(
            dimension_semantics=("parallel", "parallel"), vmem_limit_bytes=VMEM_LIMIT),
        name="retention",
    )(proj, proj, proj, proj, gn_w3, dec, xi_b, zeta_b, gch_b)


def _bf16_terms(x, n):
    terms, rest = [], np.asarray(x, np.float64)
    for _ in range(n):
        t = rest.astype(np.float32).astype(BF16).astype(np.float64)
        terms.append(t)
        rest = rest - t
    return terms


def _moba_tables(seq):
    nb = seq // MOBA_BLOCK
    assert nb <= 8
    pos = np.arange(seq)
    kaug = np.zeros((seq, MOBA_DH), np.float32)
    kaug[pos, pos // MOBA_BLOCK] = 1.0
    kaug[:, 8:11] = (pos % MOBA_BLOCK)[:, None]
    kaug[:, 11:14] = (pos // MOBA_BLOCK * MOBA_BLOCK)[:, None]
    slopes = np.exp2(-8.0 * (np.arange(MOBA_HEADS, dtype=np.float64) + 1.0) / MOBA_HEADS)
    a_terms = _bf16_terms(slopes * np.log2(np.e), 3)
    eh = np.zeros((MOBA_HEADS, MOBA_AUG, MOBA_DH), np.float32)
    eh[:, np.arange(8), np.arange(8)] = 1.0
    for t, a in enumerate(a_terms):
        eh[:, 8, 8 + t] = a
        eh[:, 8, 11 + t] = a
    return jnp.asarray(kaug, dtype=BF16), jnp.asarray(eh, dtype=BF16)


def _moba_step(q_ref, k_ref, v_ref, g_ref, kaug_ref, eh_ref, o_ref, vt_ref,
               qa_w, qt_w, ka_w, qa_r, qt_r, ka_r, s_w, mx_w, s_r, mx_r):
    bs = MOBA_BLOCK
    dh = MOBA_DH
    seq = kaug_ref.shape[0]
    nb = seq // bs
    blk = lambda j: slice(j * bs, (j + 1) * bs)
    pair = lambda i, j: slice((i * (i + 1) // 2 + j) * bs, (i * (i + 1) // 2 + j + 1) * bs)
    masked = -(2.0 ** 127)

    jidx = lax.broadcasted_iota(jnp.int32, (nb, bs), 0)
    kpos = lax.broadcasted_iota(jnp.int32, (bs, bs), 0)
    qpos = lax.broadcasted_iota(jnp.int32, (bs, bs), 1)
    causal = kpos <= qpos
    aug_tail = (lax.broadcasted_iota(jnp.int32, (MOBA_AUG - nb, bs), 0) == 0).astype(F32)
    eh = eh_ref[0]

    def query_side(mask_rows):
        aug = jnp.concatenate([mask_rows, aug_tail], axis=0).astype(BF16)
        return lax.dot_general(eh, aug, _TN, preferred_element_type=F32).astype(BF16)

    qt_w[...] = q_ref[...].T
    ka_w[:, 0:dh] = k_ref[...]
    km = jnp.concatenate(
        [jnp.sum(k_ref[blk(j), :].astype(F32), axis=0, keepdims=True) for j in range(nb)], axis=0) * (1.0 / bs)
    km_hi = km.astype(BF16)
    km_lo = (km - km_hi.astype(F32)).astype(BF16)
    km_hl = jnp.concatenate([km_hi, km_lo], axis=0)
    qt_right_all = query_side(jnp.zeros((nb, bs), F32))

    def setup_block(i):
        qt = qt_w[:, blk(i)]
        if i > MOBA_TOPK:
            gate2 = jnp.dot(km_hl, qt, preferred_element_type=F32)
            gate = gate2[:nb] + gate2[nb:]
            rank = jnp.zeros((nb, bs), F32)
            for jj in range(i):
                row = gate[jj:jj + 1, :]
                rank = rank + ((row > gate) | ((row == gate) & (jj < jidx))).astype(F32)
            keep = (rank < float(MOBA_TOPK)) | (jidx >= i)
            qt_right = query_side(jnp.where(keep, 0.0, masked))
        else:
            qt_right = qt_right_all
        qa_w[i] = qt_right

    vt_ref[0:dh, :] = v_ref[...].T

    def score_pass(i):
        qt_aug = jnp.concatenate([qt_r[:, blk(i)], qa_r[i]], axis=0)
        mx = None
        for j in range(i + 1):
            s = jnp.dot(ka_r[blk(j), :], qt_aug, preferred_element_type=F32)
            if j == i:
                s = jnp.where(causal, s, -jnp.inf)
            s_w[pair(i, j), :] = s
            mj = jnp.max(s, axis=0, keepdims=True)
            mx = mj if mx is None else jnp.maximum(mx, mj)
        mx_w[i] = mx

    def value_pass(i):
        mx = mx_r[i]
        acc = None
        for j in range(i + 1):
            p = jnp.exp2(s_r[pair(i, j), :] - mx).astype(BF16)
            pv = jnp.dot(vt_ref[:, blk(j)], p, preferred_element_type=F32)
            acc = pv if acc is None else acc + pv
        out = (acc[:dh] / acc[dh:dh + 1]).T
        g = g_ref[blk(i), :].astype(F32)
        o_ref[blk(i), :] = (out * (g * jax.nn.sigmoid(g))).astype(o_ref.dtype)

    for i in reversed(range(nb)):
        score_pass(i)
        value_pass(i)
        setup_block(i)


def _moba_kernel(q_ref, k_ref, v_ref, g_ref, kaug_ref, eh_ref, o_ref, vt_ref,
                 qa0_ref, qt0_ref, ka0_ref, s0_ref, mx0_ref, qa1_ref, qt1_ref, ka1_ref, s1_ref, mx1_ref):
    t = pl.program_id(0)
    io = (q_ref, k_ref, v_ref, g_ref, kaug_ref, eh_ref, o_ref, vt_ref)

    @pl.when(t == 0)
    def _():
        for ref in (qa1_ref, qt1_ref, ka1_ref, s0_ref, mx0_ref):
            ref[...] = jnp.zeros(ref.shape, ref.dtype)
        for ka_ref in (ka0_ref, ka1_ref):
            ka_ref[:, MOBA_DH:2 * MOBA_DH] = kaug_ref[...]
        tail_shape = (MOBA_AUG, vt_ref.shape[1])
        vt_ref[MOBA_DH:, :] = (lax.broadcasted_iota(jnp.int32, tail_shape, 0) == 0).astype(BF16)

    @pl.when(t % 2 == 0)
    def _():
        _moba_step(*io, qa0_ref, qt0_ref, ka0_ref, qa1_ref, qt1_ref, ka1_ref,
                   s1_ref, mx1_ref, s0_ref, mx0_ref)

    @pl.when(t % 2 == 1)
    def _():
        _moba_step(*io, qa1_ref, qt1_ref, ka1_ref, qa0_ref, qt0_ref, ka0_ref,
                   s0_ref, mx0_ref, s1_ref, mx1_ref)


def _moba(proj, tables, batch, seq):
    kaug, eh = tables
    bs = MOBA_BLOCK
    nb = seq // bs
    n = MOBA_HEADS * batch
    qb, kb, vb, gb = (OFF_MQ // MOBA_DH, OFF_MK // MOBA_DH, OFF_MV // MOBA_DH, OFF_MG // MOBA_DH)
    first = lambda t: jnp.minimum(t, n - 1)
    third = lambda t: jnp.clip(t - 2, 0, n - 1)
    col = lambda base, which: pl.BlockSpec(
        (seq, MOBA_DH), lambda t: (which(t) % batch, base + which(t) // batch))
    return pl.pallas_call(
        _moba_kernel,
        out_shape=jax.ShapeDtypeStruct((batch * seq, MOBA_HEADS * MOBA_DH), BF16),
        grid=(n + 2,),
        in_specs=[col(qb, first), col(kb, first), col(vb, third), col(gb, third),
                  pl.BlockSpec((seq, MOBA_DH), lambda t: (0, 0)),
                  pl.BlockSpec((1, MOBA_AUG, MOBA_DH), lambda t: (first(t) // batch, 0, 0))],
        out_specs=col(0, third),
        scratch_shapes=[pltpu.VMEM((MOBA_DH + MOBA_AUG, seq), BF16)]
        + 2 * [pltpu.VMEM((nb, MOBA_DH, bs), BF16),
               pltpu.VMEM((MOBA_DH, seq), BF16),
               pltpu.VMEM((seq, 2 * MOBA_DH), BF16),
               pltpu.VMEM((nb * (nb + 1) // 2 * bs, bs), F32),
               pltpu.VMEM((nb, 1, bs), F32)],
        compiler_params=pltpu.CompilerParams(
            dimension_semantics=("arbitrary",), vmem_limit_bytes=VMEM_LIMIT),
        name="moba",
    )(proj, proj, proj, proj, kaug, eh)


def _out_proj_kernel(ret_ref, mo_ref, gr_ref, gm_ref, x_ref, wr_ref, wm_ref, wo_ref, nw_ref,
                     *out_refs, last):
    r = jnp.dot(ret_ref[...], wr_ref[...], preferred_element_type=F32)
    m = jnp.dot(mo_ref[...], wm_ref[...], preferred_element_type=F32)
    y = jax.nn.sigmoid(gr_ref[...].astype(F32)) * r + jax.nn.sigmoid(gm_ref[...].astype(F32)) * m
    out = x_ref[...] + jnp.dot(y.astype(BF16), wo_ref[...], preferred_element_type=F32)
    ms = jnp.mean(out * out, axis=-1, keepdims=True)
    normed = out * lax.rsqrt(ms + NORM_EPS) * nw_ref[...]
    if last:
        out_refs[0][...] = normed
    else:
        out_refs[0][...] = out
        out_refs[1][...] = normed.astype(BF16)


def _out_proj(ret_g, mo_g, proj, x2, w_ret_o, w_moba_o, w_out, norm_w3, layer, norm_layer, last):
    m = x2.shape[0]
    d = D_MODEL
    tile = lambda col: pl.BlockSpec((OUT_TM, d), lambda i: (i, col))
    full = pl.BlockSpec((None, d, d), lambda i: (layer, 0, 0))
    x_shape = jax.ShapeDtypeStruct((m, d), F32)
    out_shape = x_shape if last else (x_shape, jax.ShapeDtypeStruct((m, d), BF16))
    out_specs = tile(0) if last else (tile(0), tile(0))
    return pl.pallas_call(
        functools.partial(_out_proj_kernel, last=last),
        out_shape=out_shape,
        grid=(m // OUT_TM,),
        in_specs=[tile(0), tile(0), tile(OFF_GR // d), tile(OFF_GM // d), tile(0),
                  full, full, full, pl.BlockSpec((None, 1, d), lambda i: (norm_layer, 0, 0))],
        out_specs=out_specs,
        compiler_params=pltpu.CompilerParams(
            dimension_semantics=("parallel",), vmem_limit_bytes=VMEM_LIMIT_OUT_PROJ),
        name="out_proj",
    )(ret_g, mo_g, proj, proj, x2, w_ret_o, w_moba_o, w_out, norm_w3)


def kernel(x, ln_w, w_in, ret_gn_w, w_ret_o, w_moba_o, w_out, final_norm_w):
    batch, seq, d = x.shape
    depth = w_in.shape[0]
    assert d == D_MODEL and w_in.shape[2] == D_IN
    assert seq % MOBA_BLOCK == 0 and seq % RET_CHUNK == 0 and (batch * seq) % IN_TM == 0
    ret_tables = _retention_tables()
    moba_tables = _moba_tables(seq)
    col_scale = _in_proj_col_scale()
    w_ret_b = w_ret_o.astype(BF16)
    w_moba_b = w_moba_o.astype(BF16)
    w_out_b = w_out.astype(BF16)
    ln_w3 = ln_w.reshape(depth, 1, d)
    gn_w3 = ret_gn_w.reshape(depth, 1, RET_HEADS * RET_DV)
    fn_w3 = final_norm_w.reshape(1, 1, d)
    x2 = x.reshape(batch * seq, d)
    h = _rmsnorm(x2, ln_w3, 0)
    for layer in range(depth):
        last = layer == depth - 1
        proj = _in_proj(h, w_in, col_scale, layer)
        ret_g = _retention(proj, gn_w3, layer, ret_tables, batch, seq)
        mo_g = _moba(proj, moba_tables, batch, seq)
        res = _out_proj(ret_g, mo_g, proj, x2, w_ret_b, w_moba_b, w_out_b,
                        fn_w3 if last else ln_w3, layer, 0 if last else layer + 1, last)
        if last:
            x2 = res
        else:
            x2, h = res
    return x2.reshape(batch, seq, d)
```

```python
import functools

import numpy as np
import jax
import jax.numpy as jnp
from jax import lax
from jax.experimental import pallas as pl
from jax.experimental.pallas import tpu as pltpu

D_MODEL = 1024
RET_HEADS = 4
RET_DK = 128
RET_DV = 256
RET_CHUNK = 256
RET_SEQS_PER_STEP = 4
MOBA_HEADS = 8
MOBA_DH = 128
MOBA_BLOCK = 256
MOBA_TOPK = 3
MOBA_EXP2_SCALE = MOBA_DH ** -0.5 * float(np.log2(np.e))
MOBA_AUG = 16
NORM_EPS = 1e-6

OFF_RQ, OFF_RK, OFF_RV, OFF_RG = 0, 512, 1024, 2048
OFF_MQ, OFF_MK, OFF_MV, OFF_MG = 3072, 4096, 5120, 6144
OFF_GR, OFF_GM = 7168, 8192
D_IN = 9216

NORM_TM = 2048
IN_TM = 2048
IN_TN = 1536
OUT_TM = 1024

V7X_VMEM_BYTES = 64 * 2 ** 20
VMEM_LIMIT = V7X_VMEM_BYTES * 3 // 4
VMEM_LIMIT_OUT_PROJ = V7X_VMEM_BYTES * 29 // 32

F32 = jnp.float32
BF16 = jnp.bfloat16

_NT = (((1,), (1,)), ((), ()))
_TN = (((0,), (0,)), ((), ()))


def _rmsnorm_kernel(x_ref, w_ref, o_ref):
    x = x_ref[...]
    ms = jnp.mean(x * x, axis=-1, keepdims=True)
    o_ref[...] = (x * lax.rsqrt(ms + NORM_EPS) * w_ref[...]).astype(o_ref.dtype)


def _rmsnorm(x2, ln_w3, layer):
    m, d = x2.shape
    return pl.pallas_call(
        _rmsnorm_kernel,
        out_shape=jax.ShapeDtypeStruct((m, d), BF16),
        grid=(m // NORM_TM,),
        in_specs=[pl.BlockSpec((NORM_TM, d), lambda i: (i, 0)),
                  pl.BlockSpec((None, 1, d), lambda i: (layer, 0, 0))],
        out_specs=pl.BlockSpec((NORM_TM, d), lambda i: (i, 0)),
        compiler_params=pltpu.CompilerParams(
            dimension_semantics=("parallel",), vmem_limit_bytes=VMEM_LIMIT),
        name="rmsnorm",
    )(x2, ln_w3)


def _in_proj_col_scale():
    cs = np.ones((1, D_IN), np.float32)
    cs[0, OFF_MQ:OFF_MQ + MOBA_HEADS * MOBA_DH] = MOBA_EXP2_SCALE
    cs[0, OFF_RG:OFF_RG + RET_HEADS * RET_DV] = 0.5
    cs[0, OFF_MG:OFF_MG + MOBA_HEADS * MOBA_DH] = 0.5
    return jnp.asarray(cs)


def _silu_from_half(hg):
    return hg + hg * jnp.tanh(hg)


def _in_proj_kernel(h_ref, w_ref, cs_ref, o_ref, wb_ref):
    @pl.when(pl.program_id(1) == 0)
    def _():
        wb_ref[...] = (w_ref[...] * cs_ref[...]).astype(BF16)

    o_ref[...] = jnp.dot(h_ref[...], wb_ref[...], preferred_element_type=F32).astype(o_ref.dtype)


def _in_proj(h, w_in, col_scale, layer):
    m = h.shape[0]
    return pl.pallas_call(
        _in_proj_kernel,
        out_shape=jax.ShapeDtypeStruct((m, D_IN), BF16),
        grid=(D_IN // IN_TN, m // IN_TM),
        in_specs=[
            pl.BlockSpec((IN_TM, D_MODEL), lambda j, i: (i, 0)),
            pl.BlockSpec((None, D_MODEL, IN_TN), lambda j, i: (layer, 0, j)),
            pl.BlockSpec((1, IN_TN), lambda j, i: (0, j)),
        ],
        out_specs=pl.BlockSpec((IN_TM, IN_TN), lambda j, i: (i, j)),
        scratch_shapes=[pltpu.VMEM((D_MODEL, IN_TN), BF16)],
        compiler_params=pltpu.CompilerParams(
            dimension_semantics=("parallel", "arbitrary"),
            vmem_limit_bytes=VMEM_LIMIT,
        ),
        name="in_proj",
    )(h, w_in, col_scale)


def _retention_tables():
    c = RET_CHUNK
    hh = np.arange(RET_HEADS, dtype=np.float64)
    log_g = np.log1p(-np.exp2(-5.0 - hh))
    n = np.arange(c, dtype=np.float64)
    rel = n[:, None] - n[None, :]
    scale = RET_DK ** -0.5
    dec = np.where(rel[None] >= 0, np.exp(np.maximum(rel, 0.0)[None] * log_g[:, None, None]), 0.0) * scale
    xi = np.exp((n + 1.0)[None, :] * log_g[:, None])
    zeta = np.exp((c - 1.0 - n)[None, :] * log_g[:, None]) * scale
    gch = np.exp(c * log_g)
    xi_b = np.broadcast_to(xi[:, :, None], (RET_HEADS, c, RET_DV))
    zeta_b = np.broadcast_to(zeta[:, :, None], (RET_HEADS, c, RET_DK))
    gch_b = np.broadcast_to(gch[:, None, None], (RET_HEADS, 1, RET_DV))
    f = lambda a: jnp.asarray(np.ascontiguousarray(a), dtype=F32)
    return f(dec), f(xi_b), f(zeta_b), f(gch_b)


def _retention_kernel(q_ref, k_ref, v_ref, g_ref, gnw_ref, dec_ref, xi_ref, zeta_ref, gch_ref, o_ref):
    c = RET_CHUNK
    seq = q_ref.shape[0] // RET_SEQS_PER_STEP
    n_chunks = seq // c
    states = [None] * RET_SEQS_PER_STEP
    for ci in range(n_chunks):
        for u in range(RET_SEQS_PER_STEP):
            rows = slice(u * seq + ci * c, u * seq + (ci + 1) * c)
            st = states[u]
            qc = q_ref[rows, :]
            kc = k_ref[rows, :]
            vc = v_ref[rows, :]
            inner = lax.dot_general(qc, kc, _NT, preferred_element_type=F32) * dec_ref[0]
            if st is None:
                o = jnp.dot(inner.astype(BF16), vc, preferred_element_type=F32)
            else:
                qx = (qc.astype(F32) * xi_ref[0, :, 0:RET_DK]).astype(BF16)
                lhs = jnp.concatenate([inner.astype(BF16), qx], axis=1)
                rhs = jnp.concatenate([vc, st.astype(BF16)], axis=0)
                o = jnp.dot(lhs, rhs, preferred_element_type=F32)
            if ci + 1 < n_chunks:
                kz = (kc.astype(F32) * zeta_ref[0]).astype(BF16)
                kv = lax.dot_general(kz, vc, _TN, preferred_element_type=F32)
                states[u] = kv if st is None else st * gch_ref[0] + kv
            mu = jnp.mean(o, axis=-1, keepdims=True)
            d = o - mu
            var = jnp.mean(d * d, axis=-1, keepdims=True)
            y = d * lax.rsqrt(var + NORM_EPS) * gnw_ref[...]
            gate = _silu_from_half(g_ref[rows, :].astype(F32))
            o_ref[rows, :] = (y * gate).astype(o_ref.dtype)


def _retention(proj, gn_w3, layer, tables, batch, seq):
    dec, xi_b, zeta_b, gch_b = tables
    c = RET_CHUNK
    qb, kb = OFF_RQ // RET_DK, OFF_RK // RET_DK
    vb, gb = OFF_RV // RET_DV, OFF_RG // RET_DV
    n_sub = RET_SEQS_PER_STEP
    assert batch % n_sub == 0
    rows = n_sub * seq
    return pl.pallas_call(
        _retention_kernel,
        out_shape=jax.ShapeDtypeStruct((batch * seq, RET_HEADS * RET_DV), BF16),
        grid=(RET_HEADS, batch // n_sub),
        in_specs=[
            pl.BlockSpec((rows, RET_DK), lambda h, b: (b, qb + h)),
            pl.BlockSpec((rows, RET_DK), lambda h, b: (b, kb + h)),
            pl.BlockSpec((rows, RET_DV), lambda h, b: (b, vb + h)),
            pl.BlockSpec((rows, RET_DV), lambda h, b: (b, gb + h)),
            pl.BlockSpec((None, 1, RET_DV), lambda h, b: (layer, 0, h)),
            pl.BlockSpec((1, c, c), lambda h, b: (h, 0, 0)),
            pl.BlockSpec((1, c, RET_DV), lambda h, b: (h, 0, 0)),
            pl.BlockSpec((1, c, RET_DK), lambda h, b: (h, 0, 0)),
            pl.BlockSpec((1, 1, RET_DV), lambda h, b: (h, 0, 0)),
        ],
        out_specs=pl.BlockSpec((rows, RET_DV), lambda h, b: (b, h)),
        compiler_params=pltpu.CompilerParams(
            dimension_semantics=("parallel", "parallel"), vmem_limit_bytes=VMEM_LIMIT),
        name="retention",
    )(proj, proj, proj, proj, gn_w3, dec, xi_b, zeta_b, gch_b)


def _bf16_terms(x, n):
    terms, rest = [], np.asarray(x, np.float64)
    for _ in range(n):
        t = rest.astype(np.float32).astype(BF16).astype(np.float64)
        terms.append(t)
        rest = rest - t
    return terms


def _moba_tables(seq):
    nb = seq // MOBA_BLOCK
    assert nb <= 8
    pos = np.arange(seq)
    kaug = np.zeros((seq, MOBA_DH), np.float32)
    kaug[pos, pos // MOBA_BLOCK] = 1.0
    kaug[:, 8:11] = (pos % MOBA_BLOCK)[:, None]
    kaug[:, 11:14] = (pos // MOBA_BLOCK * MOBA_BLOCK)[:, None]
    slopes = np.exp2(-8.0 * (np.arange(MOBA_HEADS, dtype=np.float64) + 1.0) / MOBA_HEADS)
    a_terms = _bf16_terms(slopes * np.log2(np.e), 3)
    eh = np.zeros((MOBA_HEADS, MOBA_AUG, MOBA_DH), np.float32)
    eh[:, np.arange(8), np.arange(8)] = 1.0
    for t, a in enumerate(a_terms):
        eh[:, 8, 8 + t] = a
        eh[:, 8, 11 + t] = a
    return jnp.asarray(kaug, dtype=BF16), jnp.asarray(eh, dtype=BF16)


def _moba_step(q_ref, k_ref, v_ref, g_ref, kaug_ref, eh_ref, o_ref, vt_ref,
               qa_w, qt_w, ka_w, qa_r, qt_r, ka_r, s_w, mx_w, s_r, mx_r):
    bs = MOBA_BLOCK
    dh = MOBA_DH
    seq = kaug_ref.shape[0]
    nb = seq // bs
    blk = lambda j: slice(j * bs, (j + 1) * bs)
    pair = lambda i, j: slice((i * (i + 1) // 2 + j) * bs, (i * (i + 1) // 2 + j + 1) * bs)
    masked = -(2.0 ** 127)

    jidx = lax.broadcasted_iota(jnp.int32, (nb, bs), 0)
    kpos = lax.broadcasted_iota(jnp.int32, (bs, bs), 0)
    qpos = lax.broadcasted_iota(jnp.int32, (bs, bs), 1)
    causal = kpos <= qpos
    aug_tail = (lax.broadcasted_iota(jnp.int32, (MOBA_AUG - nb, bs), 0) == 0).astype(F32)
    eh = eh_ref[0]

    def query_side(mask_rows):
        aug = jnp.concatenate([mask_rows, aug_tail], axis=0).astype(BF16)
        return lax.dot_general(eh, aug, _TN, preferred_element_type=F32).astype(BF16)

    qt_w[...] = q_ref[...].T
    ka_w[:, 0:dh] = k_ref[...]
    km = jnp.concatenate(
        [jnp.sum(k_ref[blk(j), :].astype(F32), axis=0, keepdims=True) for j in range(nb)], axis=0) * (1.0 / bs)
    km_hi = km.astype(BF16)
    km_lo = (km - km_hi.astype(F32)).astype(BF16)
    km_hl = jnp.concatenate([km_hi, km_lo], axis=0)
    qt_right_all = query_side(jnp.zeros((nb, bs), F32))

    def setup_block(i):
        qt = qt_w[:, blk(i)]
        if i > MOBA_TOPK:
            gate2 = jnp.dot(km_hl, qt, preferred_element_type=F32)
            gate = gate2[:nb] + gate2[nb:]
            rank = jnp.zeros((nb, bs), F32)
            for jj in range(i):
                row = gate[jj:jj + 1, :]
                rank = rank + ((row > gate) | ((row == gate) & (jj < jidx))).astype(F32)
            keep = (rank < float(MOBA_TOPK)) | (jidx >= i)
            qt_right = query_side(jnp.where(keep, 0.0, masked))
        else:
            qt_right = qt_right_all
        qa_w[i] = qt_right

    vt_ref[0:dh, :] = v_ref[...].T

    def score_pass(i):
        qt_aug = jnp.concatenate([qt_r[:, blk(i)], qa_r[i]], axis=0)
        mx = None
        for j in range(i + 1):
            s = jnp.dot(ka_r[blk(j), :], qt_aug, preferred_element_type=F32)
            if j == i:
                s = jnp.where(causal, s, -jnp.inf)
            s_w[pair(i, j), :] = s
            mj = jnp.max(s, axis=0, keepdims=True)
            mx = mj if mx is None else jnp.maximum(mx, mj)
        mx_w[i] = mx

    def value_pass(i):
        mx = mx_r[i]
        acc = None
        for j in range(i + 1):
            p = jnp.exp2(s_r[pair(i, j), :] - mx).astype(BF16)
            pv = jnp.dot(vt_ref[:, blk(j)], p, preferred_element_type=F32)
            acc = pv if acc is None else acc + pv
        out = (acc[:dh] / acc[dh:dh + 1]).T
        gate = _silu_from_half(g_ref[blk(i), :].astype(F32))
        o_ref[blk(i), :] = (out * gate).astype(o_ref.dtype)

    for i in reversed(range(nb)):
        score_pass(i)
        value_pass(i)
        setup_block(i)


def _moba_kernel(q_ref, k_ref, v_ref, g_ref, kaug_ref, eh_ref, o_ref, vt_ref,
                 qa0_ref, qt0_ref, ka0_ref, s0_ref, mx0_ref, qa1_ref, qt1_ref, ka1_ref, s1_ref, mx1_ref):
    t = pl.program_id(0)
    io = (q_ref, k_ref, v_ref, g_ref, kaug_ref, eh_ref, o_ref, vt_ref)

    @pl.when(t == 0)
    def _():
        for ref in (qa1_ref, qt1_ref, ka1_ref, s0_ref, mx0_ref):
            ref[...] = jnp.zeros(ref.shape, ref.dtype)
        for ka_ref in (ka0_ref, ka1_ref):
            ka_ref[:, MOBA_DH:2 * MOBA_DH] = kaug_ref[...]
        tail_shape = (MOBA_AUG, vt_ref.shape[1])
        vt_ref[MOBA_DH:, :] = (lax.broadcasted_iota(jnp.int32, tail_shape, 0) == 0).astype(BF16)

    @pl.when(t % 2 == 0)
    def _():
        _moba_step(*io, qa0_ref, qt0_ref, ka0_ref, qa1_ref, qt1_ref, ka1_ref,
                   s1_ref, mx1_ref, s0_ref, mx0_ref)

    @pl.when(t % 2 == 1)
    def _():
        _moba_step(*io, qa1_ref, qt1_ref, ka1_ref, qa0_ref, qt0_ref, ka0_ref,
                   s0_ref, mx0_ref, s1_ref, mx1_ref)


def _moba(proj, tables, batch, seq):
    kaug, eh = tables
    bs = MOBA_BLOCK
    nb = seq // bs
    n = MOBA_HEADS * batch
    qb, kb, vb, gb = (OFF_MQ // MOBA_DH, OFF_MK // MOBA_DH, OFF_MV // MOBA_DH, OFF_MG // MOBA_DH)
    first = lambda t: jnp.minimum(t, n - 1)
    third = lambda t: jnp.clip(t - 2, 0, n - 1)
    col = lambda base, which: pl.BlockSpec(
        (seq, MOBA_DH), lambda t: (which(t) % batch, base + which(t) // batch))
    return pl.pallas_call(
        _moba_kernel,
        out_shape=jax.ShapeDtypeStruct((batch * seq, MOBA_HEADS * MOBA_DH), BF16),
        grid=(n + 2,),
        in_specs=[col(qb, first), col(kb, first), col(vb, third), col(gb, third),
                  pl.BlockSpec((seq, MOBA_DH), lambda t: (0, 0)),
                  pl.BlockSpec((1, MOBA_AUG, MOBA_DH), lambda t: (first(t) // batch, 0, 0))],
        out_specs=col(0, third),
        scratch_shapes=[pltpu.VMEM((MOBA_DH + MOBA_AUG, seq), BF16)]
        + 2 * [pltpu.VMEM((nb, MOBA_DH, bs), BF16),
               pltpu.VMEM((MOBA_DH, seq), BF16),
               pltpu.VMEM((seq, 2 * MOBA_DH), BF16),
               pltpu.VMEM((nb * (nb + 1) // 2 * bs, bs), F32),
               pltpu.VMEM((nb, 1, bs), F32)],
        compiler_params=pltpu.CompilerParams(
            dimension_semantics=("arbitrary",), vmem_limit_bytes=VMEM_LIMIT),
        name="moba",
    )(proj, proj, proj, proj, kaug, eh)


def _out_proj_kernel(ret_ref, mo_ref, gr_ref, gm_ref, x_ref, wr_ref, wm_ref, wo_ref, nw_ref,
                     *out_refs, last):
    r = jnp.dot(ret_ref[...], wr_ref[...], preferred_element_type=F32)
    m = jnp.dot(mo_ref[...], wm_ref[...], preferred_element_type=F32)
    y = jax.nn.sigmoid(gr_ref[...].astype(F32)) * r + jax.nn.sigmoid(gm_ref[...].astype(F32)) * m
    out = x_ref[...] + jnp.dot(y.astype(BF16), wo_ref[...], preferred_element_type=F32)
    ms = jnp.mean(out * out, axis=-1, keepdims=True)
    normed = out * lax.rsqrt(ms + NORM_EPS) * nw_ref[...]
    if last:
        out_refs[0][...] = normed
    else:
        out_refs[0][...] = out
        out_refs[1][...] = normed.astype(BF16)


def _out_proj(ret_g, mo_g, proj, x2, w_ret_o, w_moba_o, w_out, norm_w3, layer, norm_layer, last):
    m = x2.shape[0]
    d = D_MODEL
    tile = lambda col: pl.BlockSpec((OUT_TM, d), lambda i: (i, col))
    full = pl.BlockSpec((None, d, d), lambda i: (layer, 0, 0))
    x_shape = jax.ShapeDtypeStruct((m, d), F32)
    out_shape = x_shape if last else (x_shape, jax.ShapeDtypeStruct((m, d), BF16))
    out_specs = tile(0) if last else (tile(0), tile(0))
    return pl.pallas_call(
        functools.partial(_out_proj_kernel, last=last),
        out_shape=out_shape,
        grid=(m // OUT_TM,),
        in_specs=[tile(0), tile(0), tile(OFF_GR // d), tile(OFF_GM // d), tile(0),
                  full, full, full, pl.BlockSpec((None, 1, d), lambda i: (norm_layer, 0, 0))],
        out_specs=out_specs,
        compiler_params=pltpu.CompilerParams(
            dimension_semantics=("parallel",), vmem_limit_bytes=VMEM_LIMIT_OUT_PROJ),
        name="out_proj",
    )(ret_g, mo_g, proj, proj, x2, w_ret_o, w_moba_o, w_out, norm_w3)


def kernel(x, ln_w, w_in, ret_gn_w, w_ret_o, w_moba_o, w_out, final_norm_w):
    batch, seq, d = x.shape
    depth = w_in.shape[0]
    assert d == D_MODEL and w_in.shape[2] == D_IN
    assert seq % MOBA_BLOCK == 0 and seq % RET_CHUNK == 0 and (batch * seq) % IN_TM == 0
    ret_tables = _retention_tables()
    moba_tables = _moba_tables(seq)
    col_scale = _in_proj_col_scale()
    w_ret_b = w_ret_o.astype(BF16)
    w_moba_b = w_moba_o.astype(BF16)
    w_out_b = w_out.astype(BF16)
    ln_w3 = ln_w.reshape(depth, 1, d)
    gn_w3 = ret_gn_w.reshape(depth, 1, RET_HEADS * RET_DV)
    fn_w3 = final_norm_w.reshape(1, 1, d)
    x2 = x.reshape(batch * seq, d)
    h = _rmsnorm(x2, ln_w3, 0)
    for layer in range(depth):
        last = layer == depth - 1
        proj = _in_proj(h, w_in, col_scale, layer)
        ret_g = _retention(proj, gn_w3, layer, ret_tables, batch, seq)
        mo_g = _moba(proj, moba_tables, batch, seq)
        res = _out_proj(ret_g, mo_g, proj, x2, w_ret_b, w_moba_b, w_out_b,
                        fn_w3 if last else ln_w3, layer, 0 if last else layer + 1, last)
        if last:
            x2 = res
        else:
            x2, h = res
    return x2.reshape(batch, seq, d)
```

```python
import functools

import numpy as np
import jax
import jax.numpy as jnp
from jax import lax
from jax.experimental import pallas as pl
from jax.experimental.pallas import tpu as pltpu

D_MODEL = 1024
RET_HEADS = 4
RET_DK = 128
RET_DV = 256
RET_CHUNK = 256
RET_SEQS_PER_STEP = 4
MOBA_HEADS = 8
MOBA_DH = 128
MOBA_BLOCK = 256
MOBA_TOPK = 3
MOBA_EXP2_SCALE = MOBA_DH ** -0.5 * float(np.log2(np.e))
MOBA_AUG = 16
NORM_EPS = 1e-6

OFF_RQ, OFF_RK, OFF_RV, OFF_RG = 0, 512, 1024, 2048
OFF_MQ, OFF_MK, OFF_MV, OFF_MG = 3072, 4096, 5120, 6144
OFF_GR, OFF_GM = 7168, 8192
D_IN = 9216

NORM_TM = 2048
IN_TM = 2048
IN_TN = 1536
OUT_TM = 1024

V7X_VMEM_BYTES = 64 * 2 ** 20
V7X_SUBLANES = 8
VMEM_LIMIT = V7X_VMEM_BYTES * 3 // 4
VMEM_LIMIT_OUT_PROJ = V7X_VMEM_BYTES * 29 // 32

F32 = jnp.float32
BF16 = jnp.bfloat16

_NT = (((1,), (1,)), ((), ()))
_TN = (((0,), (0,)), ((), ()))


def _rmsnorm_kernel(x_ref, w_ref, o_ref):
    x = x_ref[...]
    ms = jnp.mean(x * x, axis=-1, keepdims=True)
    o_ref[...] = (x * lax.rsqrt(ms + NORM_EPS) * w_ref[...]).astype(o_ref.dtype)


def _rmsnorm(x2, ln_w3, layer):
    m, d = x2.shape
    return pl.pallas_call(
        _rmsnorm_kernel,
        out_shape=jax.ShapeDtypeStruct((m, d), BF16),
        grid=(m // NORM_TM,),
        in_specs=[pl.BlockSpec((NORM_TM, d), lambda i: (i, 0)),
                  pl.BlockSpec((None, 1, d), lambda i: (layer, 0, 0))],
        out_specs=pl.BlockSpec((NORM_TM, d), lambda i: (i, 0)),
        compiler_params=pltpu.CompilerParams(
            dimension_semantics=("parallel",), vmem_limit_bytes=VMEM_LIMIT),
        name="rmsnorm",
    )(x2, ln_w3)


def _in_proj_col_scale():
    cs = np.ones((1, D_IN), np.float32)
    cs[0, OFF_MQ:OFF_MQ + MOBA_HEADS * MOBA_DH] = MOBA_EXP2_SCALE
    cs[0, OFF_RG:OFF_RG + RET_HEADS * RET_DV] = 0.5
    cs[0, OFF_MG:OFF_MG + MOBA_HEADS * MOBA_DH] = 0.5
    return jnp.asarray(cs)


def _silu_from_half(hg):
    return hg + hg * jnp.tanh(hg)


def _in_proj_kernel(h_ref, w_ref, cs_ref, o_ref, wb_ref):
    @pl.when(pl.program_id(1) == 0)
    def _():
        wb_ref[...] = (w_ref[...] * cs_ref[...]).astype(BF16)

    o_ref[...] = jnp.dot(h_ref[...], wb_ref[...], preferred_element_type=F32).astype(o_ref.dtype)


def _in_proj(h, w_in, col_scale, layer):
    m = h.shape[0]
    return pl.pallas_call(
        _in_proj_kernel,
        out_shape=jax.ShapeDtypeStruct((m, D_IN), BF16),
        grid=(D_IN // IN_TN, m // IN_TM),
        in_specs=[
            pl.BlockSpec((IN_TM, D_MODEL), lambda j, i: (i, 0)),
            pl.BlockSpec((None, D_MODEL, IN_TN), lambda j, i: (layer, 0, j)),
            pl.BlockSpec((1, IN_TN), lambda j, i: (0, j)),
        ],
        out_specs=pl.BlockSpec((IN_TM, IN_TN), lambda j, i: (i, j)),
        scratch_shapes=[pltpu.VMEM((D_MODEL, IN_TN), BF16)],
        compiler_params=pltpu.CompilerParams(
            dimension_semantics=("parallel", "arbitrary"),
            vmem_limit_bytes=VMEM_LIMIT,
        ),
        name="in_proj",
    )(h, w_in, col_scale)


def _retention_tables():
    c = RET_CHUNK
    hh = np.arange(RET_HEADS, dtype=np.float64)
    log_g = np.log1p(-np.exp2(-5.0 - hh))
    n = np.arange(c, dtype=np.float64)
    rel = n[:, None] - n[None, :]
    scale = RET_DK ** -0.5
    dec = np.where(rel[None] >= 0, np.exp(np.maximum(rel, 0.0)[None] * log_g[:, None, None]), 0.0) * scale
    xi = np.exp((n + 1.0)[None, :] * log_g[:, None])
    zeta = np.exp((c - 1.0 - n)[None, :] * log_g[:, None]) * scale
    gch = np.exp(c * log_g)
    xi_b = np.broadcast_to(xi[:, :, None], (RET_HEADS, c, RET_DV))
    zeta_b = np.broadcast_to(zeta[:, :, None], (RET_HEADS, c, RET_DK))
    gch_b = np.broadcast_to(gch[:, None, None], (RET_HEADS, 1, RET_DV))
    f = lambda a: jnp.asarray(np.ascontiguousarray(a), dtype=F32)
    return f(dec), f(xi_b), f(zeta_b), f(gch_b)


def _retention_kernel(q_ref, k_ref, v_ref, g_ref, gnw_ref, dec_ref, xi_ref, zeta_ref, gch_ref, o_ref):
    c = RET_CHUNK
    seq = q_ref.shape[0] // RET_SEQS_PER_STEP
    n_chunks = seq // c
    states = [None] * RET_SEQS_PER_STEP
    for ci in range(n_chunks):
        for u in range(RET_SEQS_PER_STEP):
            rows = slice(u * seq + ci * c, u * seq + (ci + 1) * c)
            st = states[u]
            qc = q_ref[rows, :]
            kc = k_ref[rows, :]
            vc = v_ref[rows, :]
            inner = lax.dot_general(qc, kc, _NT, preferred_element_type=F32) * dec_ref[0]
            if st is None:
                o = jnp.dot(inner.astype(BF16), vc, preferred_element_type=F32)
            else:
                qx = (qc.astype(F32) * xi_ref[0, :, 0:RET_DK]).astype(BF16)
                lhs = jnp.concatenate([inner.astype(BF16), qx], axis=1)
                rhs = jnp.concatenate([vc, st.astype(BF16)], axis=0)
                o = jnp.dot(lhs, rhs, preferred_element_type=F32)
            if ci + 1 < n_chunks:
                kz = (kc.astype(F32) * zeta_ref[0]).astype(BF16)
                kv = lax.dot_general(kz, vc, _TN, preferred_element_type=F32)
                states[u] = kv if st is None else st * gch_ref[0] + kv
            mu = jnp.mean(o, axis=-1, keepdims=True)
            d = o - mu
            var = jnp.mean(d * d, axis=-1, keepdims=True)
            y = d * lax.rsqrt(var + NORM_EPS) * gnw_ref[...]
            gate = _silu_from_half(g_ref[rows, :].astype(F32))
            o_ref[rows, :] = (y * gate).astype(o_ref.dtype)


def _retention(proj, gn_w3, layer, tables, batch, seq):
    dec, xi_b, zeta_b, gch_b = tables
    c = RET_CHUNK
    qb, kb = OFF_RQ // RET_DK, OFF_RK // RET_DK
    vb, gb = OFF_RV // RET_DV, OFF_RG // RET_DV
    n_sub = RET_SEQS_PER_STEP
    assert batch % n_sub == 0
    rows = n_sub * seq
    return pl.pallas_call(
        _retention_kernel,
        out_shape=jax.ShapeDtypeStruct((batch * seq, RET_HEADS * RET_DV), BF16),
        grid=(RET_HEADS, batch // n_sub),
        in_specs=[
            pl.BlockSpec((rows, RET_DK), lambda h, b: (b, qb + h)),
            pl.BlockSpec((rows, RET_DK), lambda h, b: (b, kb + h)),
            pl.BlockSpec((rows, RET_DV), lambda h, b: (b, vb + h)),
            pl.BlockSpec((rows, RET_DV), lambda h, b: (b, gb + h)),
            pl.BlockSpec((None, 1, RET_DV), lambda h, b: (layer, 0, h)),
            pl.BlockSpec((1, c, c), lambda h, b: (h, 0, 0)),
            pl.BlockSpec((1, c, RET_DV), lambda h, b: (h, 0, 0)),
            pl.BlockSpec((1, c, RET_DK), lambda h, b: (h, 0, 0)),
            pl.BlockSpec((1, 1, RET_DV), lambda h, b: (h, 0, 0)),
        ],
        out_specs=pl.BlockSpec((rows, RET_DV), lambda h, b: (b, h)),
        compiler_params=pltpu.CompilerParams(
            dimension_semantics=("parallel", "parallel"), vmem_limit_bytes=VMEM_LIMIT),
        name="retention",
    )(proj, proj, proj, proj, gn_w3, dec, xi_b, zeta_b, gch_b)


def _bf16_terms(x, n):
    terms, rest = [], np.asarray(x, np.float64)
    for _ in range(n):
        t = rest.astype(np.float32).astype(BF16).astype(np.float64)
        terms.append(t)
        rest = rest - t
    return terms


def _moba_tables(seq):
    nb = seq // MOBA_BLOCK
    n_terms = 3
    col_pos, col_start = nb, nb + n_terms
    assert nb < MOBA_AUG and col_start + n_terms <= MOBA_DH
    assert MOBA_BLOCK <= 256 and nb <= 256
    pos = np.arange(seq)
    kaug = np.zeros((seq, MOBA_DH), np.float32)
    kaug[pos, pos // MOBA_BLOCK] = 1.0
    kaug[:, col_pos:col_pos + n_terms] = (pos % MOBA_BLOCK)[:, None]
    kaug[:, col_start:col_start + n_terms] = (pos // MOBA_BLOCK * MOBA_BLOCK)[:, None]
    slopes = np.exp2(-8.0 * (np.arange(MOBA_HEADS, dtype=np.float64) + 1.0) / MOBA_HEADS)
    a_terms = _bf16_terms(slopes * np.log2(np.e), n_terms)
    eh = np.zeros((MOBA_HEADS, MOBA_AUG, MOBA_DH), np.float32)
    eh[:, np.arange(nb), np.arange(nb)] = 1.0
    for t, a in enumerate(a_terms):
        eh[:, nb, col_pos + t] = a
        eh[:, nb, col_start + t] = a
    return jnp.asarray(kaug, dtype=BF16), jnp.asarray(eh, dtype=BF16)


def _moba_step(q_ref, k_ref, v_ref, g_ref, kaug_ref, eh_ref, o_ref, vt_ref,
               qa_w, qt_w, ka_w, qa_r, qt_r, ka_r, s_w, mx_w, s_r, mx_r):
    bs = MOBA_BLOCK
    dh = MOBA_DH
    seq = kaug_ref.shape[0]
    nb = seq // bs
    blk = lambda j: slice(j * bs, (j + 1) * bs)
    pair = lambda i, j: slice((i * (i + 1) // 2 + j) * bs, (i * (i + 1) // 2 + j + 1) * bs)
    masked = -(2.0 ** 127)

    jidx = lax.broadcasted_iota(jnp.int32, (nb, bs), 0)
    kpos = lax.broadcasted_iota(jnp.int32, (bs, bs), 0)
    qpos = lax.broadcasted_iota(jnp.int32, (bs, bs), 1)
    causal = kpos <= qpos
    aug_tail = (lax.broadcasted_iota(jnp.int32, (MOBA_AUG - nb, bs), 0) == 0).astype(F32)
    eh = eh_ref[0]

    def query_side(mask_rows):
        aug = jnp.concatenate([mask_rows, aug_tail], axis=0).astype(BF16)
        return lax.dot_general(eh, aug, _TN, preferred_element_type=F32).astype(BF16)

    qt_w[...] = q_ref[...].T
    ka_w[:, 0:dh] = k_ref[...]
    km = jnp.concatenate(
        [jnp.sum(k_ref[blk(j), :].astype(F32), axis=0, keepdims=True) for j in range(nb)], axis=0) * (1.0 / bs)
    km_hi = km.astype(BF16)
    km_lo = (km - km_hi.astype(F32)).astype(BF16)
    km_hl = jnp.concatenate([km_hi, km_lo], axis=0)
    qt_right_all = query_side(jnp.zeros((nb, bs), F32))

    def setup_block(i):
        qt = qt_w[:, blk(i)]
        if i > MOBA_TOPK:
            gate2 = jnp.dot(km_hl, qt, preferred_element_type=F32)
            gate = gate2[:nb] + gate2[nb:]
            rank = jnp.zeros((nb, bs), F32)
            for jj in range(i):
                row = gate[jj:jj + 1, :]
                rank = rank + ((row > gate) | ((row == gate) & (jj < jidx))).astype(F32)
            keep = (rank < float(MOBA_TOPK)) | (jidx >= i)
            qt_right = query_side(jnp.where(keep, 0.0, masked))
        else:
            qt_right = qt_right_all
        qa_w[i] = qt_right

    vt_ref[0:dh, :] = v_ref[...].T

    def score_pass(i):
        qt_aug = jnp.concatenate([qt_r[:, blk(i)], qa_r[i]], axis=0)
        mx = None
        for j in range(i + 1):
            s = jnp.dot(ka_r[blk(j), :], qt_aug, preferred_element_type=F32)
            if j == i:
                s = jnp.where(causal, s, -jnp.inf)
            s_w[pair(i, j), :] = s
            mj = jnp.max(s.reshape(bs // V7X_SUBLANES, V7X_SUBLANES, bs), axis=0)
            mx = mj if mx is None else jnp.maximum(mx, mj)
        mx_w[i] = jnp.max(mx, axis=0, keepdims=True)

    def value_pass(i):
        mx = mx_r[i]
        acc = None
        for j in range(i + 1):
            p = jnp.exp2(s_r[pair(i, j), :] - mx).astype(BF16)
            pv = jnp.dot(vt_ref[:, blk(j)], p, preferred_element_type=F32)
            acc = pv if acc is None else acc + pv
        out = (acc[:dh] / acc[dh:dh + 1]).T
        gate = _silu_from_half(g_ref[blk(i), :].astype(F32))
        o_ref[blk(i), :] = (out * gate).astype(o_ref.dtype)

    for i in reversed(range(nb)):
        score_pass(i)
        value_pass(i)
        setup_block(i)


def _moba_kernel(q_ref, k_ref, v_ref, g_ref, kaug_ref, eh_ref, o_ref, vt_ref,
                 qa0_ref, qt0_ref, ka0_ref, s0_ref, mx0_ref, qa1_ref, qt1_ref, ka1_ref, s1_ref, mx1_ref):
    t = pl.program_id(0)
    io = (q_ref, k_ref, v_ref, g_ref, kaug_ref, eh_ref, o_ref, vt_ref)

    @pl.when(t == 0)
    def _():
        for ref in (qa1_ref, qt1_ref, ka1_ref, s0_ref, mx0_ref):
            ref[...] = jnp.zeros(ref.shape, ref.dtype)
        for ka_ref in (ka0_ref, ka1_ref):
            ka_ref[:, MOBA_DH:2 * MOBA_DH] = kaug_ref[...]
        tail_shape = (MOBA_AUG, vt_ref.shape[1])
        vt_ref[MOBA_DH:, :] = (lax.broadcasted_iota(jnp.int32, tail_shape, 0) == 0).astype(BF16)

    @pl.when(t % 2 == 0)
    def _():
        _moba_step(*io, qa0_ref, qt0_ref, ka0_ref, qa1_ref, qt1_ref, ka1_ref,
                   s1_ref, mx1_ref, s0_ref, mx0_ref)

    @pl.when(t % 2 == 1)
    def _():
        _moba_step(*io, qa1_ref, qt1_ref, ka1_ref, qa0_ref, qt0_ref, ka0_ref,
                   s0_ref, mx0_ref, s1_ref, mx1_ref)


def _moba(proj, tables, batch, seq):
    kaug, eh = tables
    bs = MOBA_BLOCK
    nb = seq // bs
    n = MOBA_HEADS * batch
    qb, kb, vb, gb = (OFF_MQ // MOBA_DH, OFF_MK // MOBA_DH, OFF_MV // MOBA_DH, OFF_MG // MOBA_DH)
    first = lambda t: jnp.minimum(t, n - 1)
    third = lambda t: jnp.clip(t - 2, 0, n - 1)
    col = lambda base, which: pl.BlockSpec(
        (seq, MOBA_DH), lambda t: (which(t) % batch, base + which(t) // batch))
    return pl.pallas_call(
        _moba_kernel,
        out_shape=jax.ShapeDtypeStruct((batch * seq, MOBA_HEADS * MOBA_DH), BF16),
        grid=(n + 2,),
        in_specs=[col(qb, first), col(kb, first), col(vb, third), col(gb, third),
                  pl.BlockSpec((seq, MOBA_DH), lambda t: (0, 0)),
                  pl.BlockSpec((1, MOBA_AUG, MOBA_DH), lambda t: (first(t) // batch, 0, 0))],
        out_specs=col(0, third),
        scratch_shapes=[pltpu.VMEM((MOBA_DH + MOBA_AUG, seq), BF16)]
        + 2 * [pltpu.VMEM((nb, MOBA_DH, bs), BF16),
               pltpu.VMEM((MOBA_DH, seq), BF16),
               pltpu.VMEM((seq, 2 * MOBA_DH), BF16),
               pltpu.VMEM((nb * (nb + 1) // 2 * bs, bs), F32),
               pltpu.VMEM((nb, 1, bs), F32)],
        compiler_params=pltpu.CompilerParams(
            dimension_semantics=("arbitrary",), vmem_limit_bytes=VMEM_LIMIT),
        name="moba",
    )(proj, proj, proj, proj, kaug, eh)


def _out_proj_kernel(ret_ref, mo_ref, gr_ref, gm_ref, x_ref, wr_ref, wm_ref, wo_ref, nw_ref,
                     *out_refs, last):
    r = jnp.dot(ret_ref[...], wr_ref[...], preferred_element_type=F32)
    m = jnp.dot(mo_ref[...], wm_ref[...], preferred_element_type=F32)
    y = jax.nn.sigmoid(gr_ref[...].astype(F32)) * r + jax.nn.sigmoid(gm_ref[...].astype(F32)) * m
    out = x_ref[...] + jnp.dot(y.astype(BF16), wo_ref[...], preferred_element_type=F32)
    ms = jnp.mean(out * out, axis=-1, keepdims=True)
    normed = out * lax.rsqrt(ms + NORM_EPS) * nw_ref[...]
    if last:
        out_refs[0][...] = normed
    else:
        out_refs[0][...] = out
        out_refs[1][...] = normed.astype(BF16)


def _out_proj(ret_g, mo_g, proj, x2, w_ret_o, w_moba_o, w_out, norm_w3, layer, norm_layer, last):
    m = x2.shape[0]
    d = D_MODEL
    tile = lambda col: pl.BlockSpec((OUT_TM, d), lambda i: (i, col))
    full = pl.BlockSpec((None, d, d), lambda i: (layer, 0, 0))
    x_shape = jax.ShapeDtypeStruct((m, d), F32)
    out_shape = x_shape if last else (x_shape, jax.ShapeDtypeStruct((m, d), BF16))
    out_specs = tile(0) if last else (tile(0), tile(0))
    return pl.pallas_call(
        functools.partial(_out_proj_kernel, last=last),
        out_shape=out_shape,
        grid=(m // OUT_TM,),
        in_specs=[tile(0), tile(0), tile(OFF_GR // d), tile(OFF_GM // d), tile(0),
                  full, full, full, pl.BlockSpec((None, 1, d), lambda i: (norm_layer, 0, 0))],
        out_specs=out_specs,
        compiler_params=pltpu.CompilerParams(
            dimension_semantics=("parallel",), vmem_limit_bytes=VMEM_LIMIT_OUT_PROJ),
        name="out_proj",
    )(ret_g, mo_g, proj, proj, x2, w_ret_o, w_moba_o, w_out, norm_w3)


def kernel(x, ln_w, w_in, ret_gn_w, w_ret_o, w_moba_o, w_out, final_norm_w):
    batch, seq, d = x.shape
    depth = w_in.shape[0]
    assert d == D_MODEL and w_in.shape[2] == D_IN
    assert seq % MOBA_BLOCK == 0 and seq % RET_CHUNK == 0 and (batch * seq) % IN_TM == 0
    ret_tables = _retention_tables()
    moba_tables = _moba_tables(seq)
    col_scale = _in_proj_col_scale()
    w_ret_b = w_ret_o.astype(BF16)
    w_moba_b = w_moba_o.astype(BF16)
    w_out_b = w_out.astype(BF16)
    ln_w3 = ln_w.reshape(depth, 1, d)
    gn_w3 = ret_gn_w.reshape(depth, 1, RET_HEADS * RET_DV)
    fn_w3 = final_norm_w.reshape(1, 1, d)
    x2 = x.reshape(batch * seq, d)
    h = _rmsnorm(x2, ln_w3, 0)
    for layer in range(depth):
        last = layer == depth - 1
        proj = _in_proj(h, w_in, col_scale, layer)
        ret_g = _retention(proj, gn_w3, layer, ret_tables, batch, seq)
        mo_g = _moba(proj, moba_tables, batch, seq)
        res = _out_proj(ret_g, mo_g, proj, x2, w_ret_b, w_moba_b, w_out_b,
                        fn_w3 if last else ln_w3, layer, 0 if last else layer + 1, last)
        if last:
            x2 = res
        else:
            x2, h = res
    return x2.reshape(batch, seq, d)
```

```python
import functools

import numpy as np
import jax
import jax.numpy as jnp
from jax import lax
from jax.experimental import pallas as pl
from jax.experimental.pallas import tpu as pltpu

D_MODEL = 1024
RET_HEADS = 4
RET_DK = 128
RET_DV = 256
RET_CHUNK = 256
RET_SEQS_PER_STEP = 4
MOBA_HEADS = 8
MOBA_DH = 128
MOBA_BLOCK = 256
MOBA_TOPK = 3
MOBA_EXP2_SCALE = MOBA_DH ** -0.5 * float(np.log2(np.e))
MOBA_AUG = 16
NORM_EPS = 1e-6

OFF_RQ, OFF_RK, OFF_RV, OFF_RG = 0, 512, 1024, 2048
OFF_MQ, OFF_MK, OFF_MV, OFF_MG = 3072, 4096, 5120, 6144
OFF_GR, OFF_GM = 7168, 8192
D_IN = 9216

NORM_TM = 2048
IN_TM = 2048
IN_TN = 1536
OUT_TM = 1024

V7X_VMEM_BYTES = 64 * 2 ** 20
V7X_SUBLANES = 8
VMEM_LIMIT = V7X_VMEM_BYTES * 3 // 4
VMEM_LIMIT_OUT_PROJ = V7X_VMEM_BYTES * 29 // 32

F32 = jnp.float32
BF16 = jnp.bfloat16

_NT = (((1,), (1,)), ((), ()))
_TN = (((0,), (0,)), ((), ()))


def _rmsnorm_kernel(x_ref, w_ref, o_ref):
    x = x_ref[...]
    ms = jnp.mean(x * x, axis=-1, keepdims=True)
    o_ref[...] = (x * lax.rsqrt(ms + NORM_EPS) * w_ref[...]).astype(o_ref.dtype)


def _rmsnorm(x2, ln_w3, layer):
    m, d = x2.shape
    return pl.pallas_call(
        _rmsnorm_kernel,
        out_shape=jax.ShapeDtypeStruct((m, d), BF16),
        grid=(m // NORM_TM,),
        in_specs=[pl.BlockSpec((NORM_TM, d), lambda i: (i, 0)),
                  pl.BlockSpec((None, 1, d), lambda i: (layer, 0, 0))],
        out_specs=pl.BlockSpec((NORM_TM, d), lambda i: (i, 0)),
        compiler_params=pltpu.CompilerParams(
            dimension_semantics=("parallel",), vmem_limit_bytes=VMEM_LIMIT),
        name="rmsnorm",
    )(x2, ln_w3)


def _in_proj_col_scale():
    cs = np.ones((1, D_IN), np.float32)
    cs[0, OFF_MQ:OFF_MQ + MOBA_HEADS * MOBA_DH] = MOBA_EXP2_SCALE
    cs[0, OFF_RG:OFF_RG + RET_HEADS * RET_DV] = 0.5
    cs[0, OFF_MG:OFF_MG + MOBA_HEADS * MOBA_DH] = 0.5
    return jnp.asarray(cs)


def _silu_from_half(hg):
    return hg + hg * jnp.tanh(hg)


def _in_proj_kernel(h_ref, w_ref, cs_ref, o_ref, wb_ref):
    @pl.when(pl.program_id(1) == 0)
    def _():
        wb_ref[...] = (w_ref[...] * cs_ref[...]).astype(BF16)

    o_ref[...] = jnp.dot(h_ref[...], wb_ref[...], preferred_element_type=F32).astype(o_ref.dtype)


def _in_proj(h, w_in, col_scale, layer):
    m = h.shape[0]
    return pl.pallas_call(
        _in_proj_kernel,
        out_shape=jax.ShapeDtypeStruct((m, D_IN), BF16),
        grid=(D_IN // IN_TN, m // IN_TM),
        in_specs=[
            pl.BlockSpec((IN_TM, D_MODEL), lambda j, i: (i, 0)),
            pl.BlockSpec((None, D_MODEL, IN_TN), lambda j, i: (layer, 0, j)),
            pl.BlockSpec((1, IN_TN), lambda j, i: (0, j)),
        ],
        out_specs=pl.BlockSpec((IN_TM, IN_TN), lambda j, i: (i, j)),
        scratch_shapes=[pltpu.VMEM((D_MODEL, IN_TN), BF16)],
        compiler_params=pltpu.CompilerParams(
            dimension_semantics=("parallel", "arbitrary"),
            vmem_limit_bytes=VMEM_LIMIT,
        ),
        name="in_proj",
    )(h, w_in, col_scale)


def _retention_tables():
    c = RET_CHUNK
    hh = np.arange(RET_HEADS, dtype=np.float64)
    log_g = np.log1p(-np.exp2(-5.0 - hh))
    n = np.arange(c, dtype=np.float64)
    rel = n[:, None] - n[None, :]
    scale = RET_DK ** -0.5
    dec = np.where(rel[None] >= 0, np.exp(np.maximum(rel, 0.0)[None] * log_g[:, None, None]), 0.0) * scale
    xi = np.exp((n + 1.0)[None, :] * log_g[:, None])
    zeta = np.exp((c - 1.0 - n)[None, :] * log_g[:, None]) * scale
    gch = np.exp(c * log_g)
    xi_b = np.broadcast_to(xi[:, :, None], (RET_HEADS, c, RET_DV))
    zeta_b = np.broadcast_to(zeta[:, :, None], (RET_HEADS, c, RET_DK))
    gch_b = np.broadcast_to(gch[:, None, None], (RET_HEADS, 1, RET_DV))
    f = lambda a: jnp.asarray(np.ascontiguousarray(a), dtype=F32)
    return f(dec), f(xi_b), f(zeta_b), f(gch_b)


def _retention_kernel(q_ref, k_ref, v_ref, g_ref, gnw_ref, dec_ref, xi_ref, zeta_ref, gch_ref, o_ref):
    c = RET_CHUNK
    seq = q_ref.shape[0] // RET_SEQS_PER_STEP
    n_chunks = seq // c
    states = [None] * RET_SEQS_PER_STEP
    for ci in range(n_chunks):
        for u in range(RET_SEQS_PER_STEP):
            rows = slice(u * seq + ci * c, u * seq + (ci + 1) * c)
            st = states[u]
            qc = q_ref[rows, :]
            kc = k_ref[rows, :]
            vc = v_ref[rows, :]
            inner = lax.dot_general(qc, kc, _NT, preferred_element_type=F32) * dec_ref[0]
            if st is None:
                o = jnp.dot(inner.astype(BF16), vc, preferred_element_type=F32)
            else:
                qx = (qc.astype(F32) * xi_ref[0, :, 0:RET_DK]).astype(BF16)
                lhs = jnp.concatenate([inner.astype(BF16), qx], axis=1)
                rhs = jnp.concatenate([vc, st.astype(BF16)], axis=0)
                o = jnp.dot(lhs, rhs, preferred_element_type=F32)
            if ci + 1 < n_chunks:
                kz = (kc.astype(F32) * zeta_ref[0]).astype(BF16)
                kv = lax.dot_general(kz, vc, _TN, preferred_element_type=F32)
                states[u] = kv if st is None else st * gch_ref[0] + kv
            mu = jnp.mean(o, axis=-1, keepdims=True)
            d = o - mu
            var = jnp.mean(d * d, axis=-1, keepdims=True)
            y = d * lax.rsqrt(var + NORM_EPS) * gnw_ref[...]
            gate = _silu_from_half(g_ref[rows, :].astype(F32))
            o_ref[rows, :] = (y * gate).astype(o_ref.dtype)


def _retention(proj, gn_w3, layer, tables, batch, seq):
    dec, xi_b, zeta_b, gch_b = tables
    c = RET_CHUNK
    qb, kb = OFF_RQ // RET_DK, OFF_RK // RET_DK
    vb, gb = OFF_RV // RET_DV, OFF_RG // RET_DV
    n_sub = RET_SEQS_PER_STEP
    assert batch % n_sub == 0
    rows = n_sub * seq
    return pl.pallas_call(
        _retention_kernel,
        out_shape=jax.ShapeDtypeStruct((batch * seq, RET_HEADS * RET_DV), BF16),
        grid=(RET_HEADS, batch // n_sub),
        in_specs=[
            pl.BlockSpec((rows, RET_DK), lambda h, b: (b, qb + h)),
            pl.BlockSpec((rows, RET_DK), lambda h, b: (b, kb + h)),
            pl.BlockSpec((rows, RET_DV), lambda h, b: (b, vb + h)),
            pl.BlockSpec((rows, RET_DV), lambda h, b: (b, gb + h)),
            pl.BlockSpec((None, 1, RET_DV), lambda h, b: (layer, 0, h)),
            pl.BlockSpec((1, c, c), lambda h, b: (h, 0, 0)),
            pl.BlockSpec((1, c, RET_DV), lambda h, b: (h, 0, 0)),
            pl.BlockSpec((1, c, RET_DK), lambda h, b: (h, 0, 0)),
            pl.BlockSpec((1, 1, RET_DV), lambda h, b: (h, 0, 0)),
        ],
        out_specs=pl.BlockSpec((rows, RET_DV), lambda h, b: (b, h)),
        compiler_params=pltpu.CompilerParams(
            dimension_semantics=("parallel", "parallel"), vmem_limit_bytes=VMEM_LIMIT),
        name="retention",
    )(proj, proj, proj, proj, gn_w3, dec, xi_b, zeta_b, gch_b)


def _bf16_terms(x, n):
    terms, rest = [], np.asarray(x, np.float64)
    for _ in range(n):
        t = rest.astype(np.float32).astype(BF16).astype(np.float64)
        terms.append(t)
        rest = rest - t
    return terms


def _moba_tables(seq):
    nb = seq // MOBA_BLOCK
    n_terms = 3
    col_pos, col_start = nb, nb + n_terms
    assert nb < MOBA_AUG and col_start + n_terms <= MOBA_DH
    assert MOBA_BLOCK <= 256 and nb <= 256
    pos = np.arange(seq)
    kaug = np.zeros((seq, MOBA_DH), np.float32)
    kaug[pos, pos // MOBA_BLOCK] = 1.0
    kaug[:, col_pos:col_pos + n_terms] = (pos % MOBA_BLOCK)[:, None]
    kaug[:, col_start:col_start + n_terms] = (pos // MOBA_BLOCK * MOBA_BLOCK)[:, None]
    slopes = np.exp2(-8.0 * (np.arange(MOBA_HEADS, dtype=np.float64) + 1.0) / MOBA_HEADS)
    a_terms = _bf16_terms(slopes * np.log2(np.e), n_terms)
    eh = np.zeros((MOBA_HEADS, MOBA_AUG, MOBA_DH), np.float32)
    eh[:, np.arange(nb), np.arange(nb)] = 1.0
    for t, a in enumerate(a_terms):
        eh[:, nb, col_pos + t] = a
        eh[:, nb, col_start + t] = a
    return jnp.asarray(kaug, dtype=BF16), jnp.asarray(eh, dtype=BF16)


def _moba_step(q_ref, k_ref, v_ref, g_ref, kaug_ref, eh_ref, o_ref, vt_ref,
               qa_w, qt_w, ka_w, qa_r, qt_r, ka_r, s_w, mx_w, s_r, mx_r):
    bs = MOBA_BLOCK
    dh = MOBA_DH
    seq = kaug_ref.shape[0]
    nb = seq // bs
    blk = lambda j: slice(j * bs, (j + 1) * bs)
    pair = lambda i, j: slice((i * (i + 1) // 2 + j) * bs, (i * (i + 1) // 2 + j + 1) * bs)
    masked = -(2.0 ** 127)

    jidx = lax.broadcasted_iota(jnp.int32, (nb, bs), 0)
    kpos = lax.broadcasted_iota(jnp.int32, (bs, bs), 0)
    qpos = lax.broadcasted_iota(jnp.int32, (bs, bs), 1)
    causal = kpos <= qpos
    aug_tail = (lax.broadcasted_iota(jnp.int32, (MOBA_AUG - nb, bs), 0) == 0).astype(F32)
    eh = eh_ref[0]

    def query_side(mask_rows):
        aug = jnp.concatenate([mask_rows, aug_tail], axis=0).astype(BF16)
        return lax.dot_general(eh, aug, _TN, preferred_element_type=F32).astype(BF16)

    qt_w[...] = q_ref[...].T
    ka_w[:, 0:dh] = k_ref[...]
    km = jnp.concatenate(
        [jnp.sum(k_ref[blk(j), :].astype(F32), axis=0, keepdims=True) for j in range(nb)], axis=0) * (1.0 / bs)
    km_hi = km.astype(BF16)
    km_lo = (km - km_hi.astype(F32)).astype(BF16)
    km_hl = jnp.concatenate([km_hi, km_lo], axis=0)
    qt_right_all = query_side(jnp.zeros((nb, bs), F32))

    def setup_block(i):
        qt = qt_w[:, blk(i)]
        if i > MOBA_TOPK:
            gate2 = jnp.dot(km_hl, qt, preferred_element_type=F32)
            gate = gate2[:nb] + gate2[nb:]
            rank = jnp.zeros((nb, bs), F32)
            for jj in range(i):
                row = gate[jj:jj + 1, :]
                rank = rank + ((row > gate) | ((row == gate) & (jj < jidx))).astype(F32)
            keep = (rank < float(MOBA_TOPK)) | (jidx >= i)
            qt_right = query_side(jnp.where(keep, 0.0, masked))
        else:
            qt_right = qt_right_all
        qa_w[i] = qt_right

    vt_ref[0:dh, :] = v_ref[...].T

    def score_pass(i):
        qt_aug = jnp.concatenate([qt_r[:, blk(i)], qa_r[i]], axis=0)
        mx = None
        for j in range(i + 1):
            s = jnp.dot(ka_r[blk(j), :], qt_aug, preferred_element_type=F32)
            if j == i:
                s = jnp.where(causal, s, -jnp.inf)
            s_w[pair(i, j), :] = s
            mj = jnp.max(s.reshape(bs // V7X_SUBLANES, V7X_SUBLANES, bs), axis=0)
            mx = mj if mx is None else jnp.maximum(mx, mj)
        mx_w[i] = jnp.max(mx, axis=0, keepdims=True)

    def value_pass(i):
        mx = mx_r[i]
        acc = None
        for j in range(i + 1):
            if j == i:
                hb = bs // 2
                r0 = pair(i, j).start
                top = jnp.exp2(s_r[r0:r0 + hb, :] - mx).astype(BF16)
                right = jnp.exp2(s_r[r0 + hb:r0 + bs, hb:bs] - mx[:, hb:bs]).astype(BF16)
                p = jnp.concatenate([top, jnp.concatenate([jnp.zeros((hb, hb), BF16), right], axis=1)], axis=0)
            else:
                p = jnp.exp2(s_r[pair(i, j), :] - mx).astype(BF16)
            pv = jnp.dot(vt_ref[:, blk(j)], p, preferred_element_type=F32)
            acc = pv if acc is None else acc + pv
        out = (acc[:dh] / acc[dh:dh + 1]).T
        gate = _silu_from_half(g_ref[blk(i), :].astype(F32))
        o_ref[blk(i), :] = (out * gate).astype(o_ref.dtype)

    for i in reversed(range(nb)):
        score_pass(i)
        value_pass(i)
        setup_block(i)


def _moba_kernel(q_ref, k_ref, v_ref, g_ref, kaug_ref, eh_ref, o_ref, vt_ref,
                 qa0_ref, qt0_ref, ka0_ref, s0_ref, mx0_ref, qa1_ref, qt1_ref, ka1_ref, s1_ref, mx1_ref):
    t = pl.program_id(0)
    io = (q_ref, k_ref, v_ref, g_ref, kaug_ref, eh_ref, o_ref, vt_ref)

    @pl.when(t == 0)
    def _():
        for ref in (qa1_ref, qt1_ref, ka1_ref, s0_ref, mx0_ref):
            ref[...] = jnp.zeros(ref.shape, ref.dtype)
        for ka_ref in (ka0_ref, ka1_ref):
            ka_ref[:, MOBA_DH:2 * MOBA_DH] = kaug_ref[...]
        tail_shape = (MOBA_AUG, vt_ref.shape[1])
        vt_ref[MOBA_DH:, :] = (lax.broadcasted_iota(jnp.int32, tail_shape, 0) == 0).astype(BF16)

    @pl.when(t % 2 == 0)
    def _():
        _moba_step(*io, qa0_ref, qt0_ref, ka0_ref, qa1_ref, qt1_ref, ka1_ref,
                   s1_ref, mx1_ref, s0_ref, mx0_ref)

    @pl.when(t % 2 == 1)
    def _():
        _moba_step(*io, qa1_ref, qt1_ref, ka1_ref, qa0_ref, qt0_ref, ka0_ref,
                   s0_ref, mx0_ref, s1_ref, mx1_ref)


def _moba(proj, tables, batch, seq):
    kaug, eh = tables
    bs = MOBA_BLOCK
    nb = seq // bs
    n = MOBA_HEADS * batch
    qb, kb, vb, gb = (OFF_MQ // MOBA_DH, OFF_MK // MOBA_DH, OFF_MV // MOBA_DH, OFF_MG // MOBA_DH)
    first = lambda t: jnp.minimum(t, n - 1)
    third = lambda t: jnp.clip(t - 2, 0, n - 1)
    col = lambda base, which: pl.BlockSpec(
        (seq, MOBA_DH), lambda t: (which(t) % batch, base + which(t) // batch))
    return pl.pallas_call(
        _moba_kernel,
        out_shape=jax.ShapeDtypeStruct((batch * seq, MOBA_HEADS * MOBA_DH), BF16),
        grid=(n + 2,),
        in_specs=[col(qb, first), col(kb, first), col(vb, third), col(gb, third),
                  pl.BlockSpec((seq, MOBA_DH), lambda t: (0, 0)),
                  pl.BlockSpec((1, MOBA_AUG, MOBA_DH), lambda t: (first(t) // batch, 0, 0))],
        out_specs=col(0, third),
        scratch_shapes=[pltpu.VMEM((MOBA_DH + MOBA_AUG, seq), BF16)]
        + 2 * [pltpu.VMEM((nb, MOBA_DH, bs), BF16),
               pltpu.VMEM((MOBA_DH, seq), BF16),
               pltpu.VMEM((seq, 2 * MOBA_DH), BF16),
               pltpu.VMEM((nb * (nb + 1) // 2 * bs, bs), F32),
               pltpu.VMEM((nb, 1, bs), F32)],
        compiler_params=pltpu.CompilerParams(
            dimension_semantics=("arbitrary",), vmem_limit_bytes=VMEM_LIMIT),
        name="moba",
    )(proj, proj, proj, proj, kaug, eh)


def _out_proj_kernel(ret_ref, mo_ref, gr_ref, gm_ref, x_ref, wr_ref, wm_ref, wo_ref, nw_ref,
                     *out_refs, last):
    r = jnp.dot(ret_ref[...], wr_ref[...], preferred_element_type=F32)
    m = jnp.dot(mo_ref[...], wm_ref[...], preferred_element_type=F32)
    y = jax.nn.sigmoid(gr_ref[...].astype(F32)) * r + jax.nn.sigmoid(gm_ref[...].astype(F32)) * m
    out = x_ref[...] + jnp.dot(y.astype(BF16), wo_ref[...], preferred_element_type=F32)
    ms = jnp.mean(out * out, axis=-1, keepdims=True)
    normed = out * lax.rsqrt(ms + NORM_EPS) * nw_ref[...]
    if last:
        out_refs[0][...] = normed
    else:
        out_refs[0][...] = out
        out_refs[1][...] = normed.astype(BF16)


def _out_proj(ret_g, mo_g, proj, x2, w_ret_o, w_moba_o, w_out, norm_w3, layer, norm_layer, last):
    m = x2.shape[0]
    d = D_MODEL
    tile = lambda col: pl.BlockSpec((OUT_TM, d), lambda i: (i, col))
    full = pl.BlockSpec((None, d, d), lambda i: (layer, 0, 0))
    x_shape = jax.ShapeDtypeStruct((m, d), F32)
    out_shape = x_shape if last else (x_shape, jax.ShapeDtypeStruct((m, d), BF16))
    out_specs = tile(0) if last else (tile(0), tile(0))
    return pl.pallas_call(
        functools.partial(_out_proj_kernel, last=last),
        out_shape=out_shape,
        grid=(m // OUT_TM,),
        in_specs=[tile(0), tile(0), tile(OFF_GR // d), tile(OFF_GM // d), tile(0),
                  full, full, full, pl.BlockSpec((None, 1, d), lambda i: (norm_layer, 0, 0))],
        out_specs=out_specs,
        compiler_params=pltpu.CompilerParams(
            dimension_semantics=("parallel",), vmem_limit_bytes=VMEM_LIMIT_OUT_PROJ),
        name="out_proj",
    )(ret_g, mo_g, proj, proj, x2, w_ret_o, w_moba_o, w_out, norm_w3)


def kernel(x, ln_w, w_in, ret_gn_w, w_ret_o, w_moba_o, w_out, final_norm_w):
    batch, seq, d = x.shape
    depth = w_in.shape[0]
    assert d == D_MODEL and w_in.shape[2] == D_IN
    assert seq % MOBA_BLOCK == 0 and seq % RET_CHUNK == 0 and (batch * seq) % IN_TM == 0
    ret_tables = _retention_tables()
    moba_tables = _moba_tables(seq)
    col_scale = _in_proj_col_scale()
    w_ret_b = w_ret_o.astype(BF16)
    w_moba_b = w_moba_o.astype(BF16)
    w_out_b = w_out.astype(BF16)
    ln_w3 = ln_w.reshape(depth, 1, d)
    gn_w3 = ret_gn_w.reshape(depth, 1, RET_HEADS * RET_DV)
    fn_w3 = final_norm_w.reshape(1, 1, d)
    x2 = x.reshape(batch * seq, d)
    h = _rmsnorm(x2, ln_w3, 0)
    for layer in range(depth):
        last = layer == depth - 1
        proj = _in_proj(h, w_in, col_scale, layer)
        ret_g = _retention(proj, gn_w3, layer, ret_tables, batch, seq)
        mo_g = _moba(proj, moba_tables, batch, seq)
        res = _out_proj(ret_g, mo_g, proj, x2, w_ret_b, w_moba_b, w_out_b,
                        fn_w3 if last else ln_w3, layer, 0 if last else layer + 1, last)
        if last:
            x2 = res
        else:
            x2, h = res
    return x2.reshape(batch, seq, d)
```

```python
import functools

import numpy as np
import jax
import jax.numpy as jnp
from jax import lax
from jax.experimental import pallas as pl
from jax.experimental.pallas import tpu as pltpu

D_MODEL = 1024
RET_HEADS = 4
RET_DK = 128
RET_DV = 256
RET_CHUNK = 256
RET_SEQS_PER_STEP = 4
MOBA_HEADS = 8
MOBA_DH = 128
MOBA_BLOCK = 256
MOBA_TOPK = 3
MOBA_EXP2_SCALE = MOBA_DH ** -0.5 * float(np.log2(np.e))
MOBA_AUG = 16
NORM_EPS = 1e-6

OFF_RQ, OFF_RK, OFF_RV, OFF_RG = 0, 512, 1024, 2048
OFF_MQ, OFF_MK, OFF_MV, OFF_MG = 3072, 4096, 5120, 6144
OFF_GR, OFF_GM = 7168, 8192
D_IN = 9216
PROJ_LANES = 128

NORM_TM = 2048
IN_TM = 2048
IN_TN = 1536
OUT_TM = 1024

V7X_VMEM_BYTES = 64 * 2 ** 20
V7X_SUBLANES = 8
VMEM_LIMIT = V7X_VMEM_BYTES * 3 // 4
VMEM_LIMIT_OUT_PROJ = V7X_VMEM_BYTES * 29 // 32

F32 = jnp.float32
BF16 = jnp.bfloat16

_NT = (((1,), (1,)), ((), ()))
_TN = (((0,), (0,)), ((), ()))


def _rmsnorm_kernel(x_ref, w_ref, o_ref):
    x = x_ref[...]
    ms = jnp.mean(x * x, axis=-1, keepdims=True)
    o_ref[...] = (x * lax.rsqrt(ms + NORM_EPS) * w_ref[...]).astype(o_ref.dtype)


def _rmsnorm(x2, ln_w3, layer):
    m, d = x2.shape
    return pl.pallas_call(
        _rmsnorm_kernel,
        out_shape=jax.ShapeDtypeStruct((m, d), BF16),
        grid=(m // NORM_TM,),
        in_specs=[pl.BlockSpec((NORM_TM, d), lambda i: (i, 0)),
                  pl.BlockSpec((None, 1, d), lambda i: (layer, 0, 0))],
        out_specs=pl.BlockSpec((NORM_TM, d), lambda i: (i, 0)),
        compiler_params=pltpu.CompilerParams(
            dimension_semantics=("parallel",), vmem_limit_bytes=VMEM_LIMIT),
        name="rmsnorm",
    )(x2, ln_w3)


def _in_proj_col_scale():
    cs = np.ones((1, D_IN), np.float32)
    cs[0, OFF_MQ:OFF_MQ + MOBA_HEADS * MOBA_DH] = MOBA_EXP2_SCALE
    cs[0, OFF_RG:OFF_RG + RET_HEADS * RET_DV] = 0.5
    cs[0, OFF_MG:OFF_MG + MOBA_HEADS * MOBA_DH] = 0.5
    return jnp.asarray(cs)


def _silu_from_half(hg):
    return hg + hg * jnp.tanh(hg)


def _in_proj_kernel(h_ref, w_ref, cs_ref, o_ref, wb_ref):
    @pl.when(pl.program_id(1) == 0)
    def _():
        wb_ref[...] = (w_ref[...] * cs_ref[...]).astype(BF16)

    res = jnp.dot(h_ref[...], wb_ref[...], preferred_element_type=F32).astype(o_ref.dtype)
    for c in range(o_ref.shape[0]):
        o_ref[c] = res[:, c * PROJ_LANES:(c + 1) * PROJ_LANES]


def _in_proj(h, w_in, col_scale, layer):
    m = h.shape[0]
    return pl.pallas_call(
        _in_proj_kernel,
        out_shape=jax.ShapeDtypeStruct((D_IN // PROJ_LANES, m, PROJ_LANES), BF16),
        grid=(D_IN // IN_TN, m // IN_TM),
        in_specs=[
            pl.BlockSpec((IN_TM, D_MODEL), lambda j, i: (i, 0)),
            pl.BlockSpec((None, D_MODEL, IN_TN), lambda j, i: (layer, 0, j)),
            pl.BlockSpec((1, IN_TN), lambda j, i: (0, j)),
        ],
        out_specs=pl.BlockSpec((IN_TN // PROJ_LANES, IN_TM, PROJ_LANES), lambda j, i: (j, i, 0)),
        scratch_shapes=[pltpu.VMEM((D_MODEL, IN_TN), BF16)],
        compiler_params=pltpu.CompilerParams(
            dimension_semantics=("parallel", "arbitrary"),
            vmem_limit_bytes=VMEM_LIMIT,
        ),
        name="in_proj",
    )(h, w_in, col_scale)


def _retention_tables():
    c = RET_CHUNK
    hh = np.arange(RET_HEADS, dtype=np.float64)
    log_g = np.log1p(-np.exp2(-5.0 - hh))
    n = np.arange(c, dtype=np.float64)
    rel = n[:, None] - n[None, :]
    scale = RET_DK ** -0.5
    dec = np.where(rel[None] >= 0, np.exp(np.maximum(rel, 0.0)[None] * log_g[:, None, None]), 0.0) * scale
    xi = np.exp((n + 1.0)[None, :] * log_g[:, None])
    zeta = np.exp((c - 1.0 - n)[None, :] * log_g[:, None]) * scale
    gch = np.exp(c * log_g)
    xi_b = np.broadcast_to(xi[:, :, None], (RET_HEADS, c, RET_DV))
    zeta_b = np.broadcast_to(zeta[:, :, None], (RET_HEADS, c, RET_DK))
    gch_b = np.broadcast_to(gch[:, None, None], (RET_HEADS, 1, RET_DV))
    f = lambda a: jnp.asarray(np.ascontiguousarray(a), dtype=F32)
    return f(dec), f(xi_b), f(zeta_b), f(gch_b)


def _retention_kernel(q_ref, k_ref, v_ref, g_ref, gnw_ref, dec_ref, xi_ref, zeta_ref, gch_ref, o_ref):
    c = RET_CHUNK
    seq = q_ref.shape[0] // RET_SEQS_PER_STEP
    n_chunks = seq // c
    states = [None] * RET_SEQS_PER_STEP
    for ci in range(n_chunks):
        for u in range(RET_SEQS_PER_STEP):
            rows = slice(u * seq + ci * c, u * seq + (ci + 1) * c)
            st = states[u]
            qc = q_ref[rows, :]
            kc = k_ref[rows, :]
            vc = jnp.concatenate([v_ref[w, rows, :] for w in range(v_ref.shape[0])], axis=1)
            inner = lax.dot_general(qc, kc, _NT, preferred_element_type=F32) * dec_ref[0]
            if st is None:
                o = jnp.dot(inner.astype(BF16), vc, preferred_element_type=F32)
            else:
                qx = (qc.astype(F32) * xi_ref[0, :, 0:RET_DK]).astype(BF16)
                lhs = jnp.concatenate([inner.astype(BF16), qx], axis=1)
                rhs = jnp.concatenate([vc, st.astype(BF16)], axis=0)
                o = jnp.dot(lhs, rhs, preferred_element_type=F32)
            if ci + 1 < n_chunks:
                kz = (kc.astype(F32) * zeta_ref[0]).astype(BF16)
                kv = lax.dot_general(kz, vc, _TN, preferred_element_type=F32)
                states[u] = kv if st is None else st * gch_ref[0] + kv
            mu = jnp.mean(o, axis=-1, keepdims=True)
            d = o - mu
            var = jnp.mean(d * d, axis=-1, keepdims=True)
            y = d * lax.rsqrt(var + NORM_EPS) * gnw_ref[...]
            g = jnp.concatenate([g_ref[w, rows, :] for w in range(g_ref.shape[0])], axis=1)
            o_ref[rows, :] = (y * _silu_from_half(g.astype(F32))).astype(o_ref.dtype)


def _retention(proj, gn_w3, layer, tables, batch, seq):
    dec, xi_b, zeta_b, gch_b = tables
    c = RET_CHUNK
    qb, kb = OFF_RQ // RET_DK, OFF_RK // RET_DK
    vb, gb = OFF_RV // RET_DV, OFF_RG // RET_DV
    n_sub = RET_SEQS_PER_STEP
    assert batch % n_sub == 0
    rows = n_sub * seq
    return pl.pallas_call(
        _retention_kernel,
        out_shape=jax.ShapeDtypeStruct((batch * seq, RET_HEADS * RET_DV), BF16),
        grid=(RET_HEADS, batch // n_sub),
        in_specs=[
            pl.BlockSpec((None, rows, RET_DK), lambda h, b: (qb + h, b, 0)),
            pl.BlockSpec((None, rows, RET_DK), lambda h, b: (kb + h, b, 0)),
            pl.BlockSpec((RET_DV // PROJ_LANES, rows, PROJ_LANES), lambda h, b: (vb + h, b, 0)),
            pl.BlockSpec((RET_DV // PROJ_LANES, rows, PROJ_LANES), lambda h, b: (gb + h, b, 0)),
            pl.BlockSpec((None, 1, RET_DV), lambda h, b: (layer, 0, h)),
            pl.BlockSpec((1, c, c), lambda h, b: (h, 0, 0)),
            pl.BlockSpec((1, c, RET_DV), lambda h, b: (h, 0, 0)),
            pl.BlockSpec((1, c, RET_DK), lambda h, b: (h, 0, 0)),
            pl.BlockSpec((1, 1, RET_DV), lambda h, b: (h, 0, 0)),
        ],
        out_specs=pl.BlockSpec((rows, RET_DV), lambda h, b: (b, h)),
        compiler_params=pltpu.CompilerParams(
            dimension_semantics=("parallel", "parallel"), vmem_limit_bytes=VMEM_LIMIT),
        name="retention",
    )(proj, proj, proj, proj, gn_w3, dec, xi_b, zeta_b, gch_b)


def _bf16_terms(x, n):
    terms, rest = [], np.asarray(x, np.float64)
    for _ in range(n):
        t = rest.astype(np.float32).astype(BF16).astype(np.float64)
        terms.append(t)
        rest = rest - t
    return terms


def _moba_tables(seq):
    nb = seq // MOBA_BLOCK
    n_terms = 3
    col_pos, col_start = nb, nb + n_terms
    assert nb < MOBA_AUG and col_start + n_terms <= MOBA_DH
    assert MOBA_BLOCK <= 256 and nb <= 256
    pos = np.arange(seq)
    kaug = np.zeros((seq, MOBA_DH), np.float32)
    kaug[pos, pos // MOBA_BLOCK] = 1.0
    kaug[:, col_pos:col_pos + n_terms] = (pos % MOBA_BLOCK)[:, None]
    kaug[:, col_start:col_start + n_terms] = (pos // MOBA_BLOCK * MOBA_BLOCK)[:, None]
    slopes = np.exp2(-8.0 * (np.arange(MOBA_HEADS, dtype=np.float64) + 1.0) / MOBA_HEADS)
    a_terms = _bf16_terms(slopes * np.log2(np.e), n_terms)
    eh = np.zeros((MOBA_HEADS, MOBA_AUG, MOBA_DH), np.float32)
    eh[:, np.arange(nb), np.arange(nb)] = 1.0
    for t, a in enumerate(a_terms):
        eh[:, nb, col_pos + t] = a
        eh[:, nb, col_start + t] = a
    return jnp.asarray(kaug, dtype=BF16), jnp.asarray(eh, dtype=BF16)


def _moba_step(q_ref, k_ref, v_ref, g_ref, kaug_ref, eh_ref, o_ref, vt_ref,
               qa_w, qt_w, ka_w, qa_r, qt_r, ka_r, s_w, mx_w, s_r, mx_r):
    bs = MOBA_BLOCK
    dh = MOBA_DH
    seq = kaug_ref.shape[0]
    nb = seq // bs
    blk = lambda j: slice(j * bs, (j + 1) * bs)
    pair = lambda i, j: slice((i * (i + 1) // 2 + j) * bs, (i * (i + 1) // 2 + j + 1) * bs)
    masked = -(2.0 ** 127)

    jidx = lax.broadcasted_iota(jnp.int32, (nb, bs), 0)
    kpos = lax.broadcasted_iota(jnp.int32, (bs, bs), 0)
    qpos = lax.broadcasted_iota(jnp.int32, (bs, bs), 1)
    causal = kpos <= qpos
    aug_tail = (lax.broadcasted_iota(jnp.int32, (MOBA_AUG - nb, bs), 0) == 0).astype(F32)
    eh = eh_ref[0]

    def query_side(mask_rows):
        aug = jnp.concatenate([mask_rows, aug_tail], axis=0).astype(BF16)
        return lax.dot_general(eh, aug, _TN, preferred_element_type=F32).astype(BF16)

    qt_w[...] = q_ref[...].T
    ka_w[:, 0:dh] = k_ref[...]
    km = jnp.concatenate(
        [jnp.sum(k_ref[blk(j), :].astype(F32), axis=0, keepdims=True) for j in range(nb)], axis=0) * (1.0 / bs)
    km_hi = km.astype(BF16)
    km_lo = (km - km_hi.astype(F32)).astype(BF16)
    km_hl = jnp.concatenate([km_hi, km_lo], axis=0)
    qt_right_all = query_side(jnp.zeros((nb, bs), F32))

    def setup_block(i):
        qt = qt_w[:, blk(i)]
        if i > MOBA_TOPK:
            gate2 = jnp.dot(km_hl, qt, preferred_element_type=F32)
            gate = gate2[:nb] + gate2[nb:]
            rank = jnp.zeros((nb, bs), F32)
            for jj in range(i):
                row = gate[jj:jj + 1, :]
                rank = rank + ((row > gate) | ((row == gate) & (jj < jidx))).astype(F32)
            keep = (rank < float(MOBA_TOPK)) | (jidx >= i)
            qt_right = query_side(jnp.where(keep, 0.0, masked))
        else:
            qt_right = qt_right_all
        qa_w[i] = qt_right

    vt_ref[0:dh, :] = v_ref[...].T

    def score_pass(i):
        qt_aug = jnp.concatenate([qt_r[:, blk(i)], qa_r[i]], axis=0)
        mx = None
        for j in range(i + 1):
            s = jnp.dot(ka_r[blk(j), :], qt_aug, preferred_element_type=F32)
            if j == i:
                s = jnp.where(causal, s, -jnp.inf)
            s_w[pair(i, j), :] = s
            mj = jnp.max(s.reshape(bs // V7X_SUBLANES, V7X_SUBLANES, bs), axis=0)
            mx = mj if mx is None else jnp.maximum(mx, mj)
        mx_w[i] = jnp.max(mx, axis=0, keepdims=True)

    def value_pass(i):
        mx = mx_r[i]
        acc = None
        for j in range(i + 1):
            if j == i:
                hb = bs // 2
                r0 = pair(i, j).start
                top = jnp.exp2(s_r[r0:r0 + hb, :] - mx).astype(BF16)
                right = jnp.exp2(s_r[r0 + hb:r0 + bs, hb:bs] - mx[:, hb:bs]).astype(BF16)
                p = jnp.concatenate([top, jnp.concatenate([jnp.zeros((hb, hb), BF16), right], axis=1)], axis=0)
            else:
                p = jnp.exp2(s_r[pair(i, j), :] - mx).astype(BF16)
            pv = jnp.dot(vt_ref[:, blk(j)], p, preferred_element_type=F32)
            acc = pv if acc is None else acc + pv
        out = (acc[:dh] / acc[dh:dh + 1]).T
        gate = _silu_from_half(g_ref[blk(i), :].astype(F32))
        o_ref[blk(i), :] = (out * gate).astype(o_ref.dtype)

    for i in reversed(range(nb)):
        score_pass(i)
        value_pass(i)
        setup_block(i)


def _moba_kernel(q_ref, k_ref, v_ref, g_ref, kaug_ref, eh_ref, o_ref, vt_ref,
                 qa0_ref, qt0_ref, ka0_ref, s0_ref, mx0_ref, qa1_ref, qt1_ref, ka1_ref, s1_ref, mx1_ref):
    t = pl.program_id(0)
    io = (q_ref, k_ref, v_ref, g_ref, kaug_ref, eh_ref, o_ref, vt_ref)

    @pl.when(t == 0)
    def _():
        for ref in (qa1_ref, qt1_ref, ka1_ref, s0_ref, mx0_ref):
            ref[...] = jnp.zeros(ref.shape, ref.dtype)
        for ka_ref in (ka0_ref, ka1_ref):
            ka_ref[:, MOBA_DH:2 * MOBA_DH] = kaug_ref[...]
        tail_shape = (MOBA_AUG, vt_ref.shape[1])
        vt_ref[MOBA_DH:, :] = (lax.broadcasted_iota(jnp.int32, tail_shape, 0) == 0).astype(BF16)

    @pl.when(t % 2 == 0)
    def _():
        _moba_step(*io, qa0_ref, qt0_ref, ka0_ref, qa1_ref, qt1_ref, ka1_ref,
                   s1_ref, mx1_ref, s0_ref, mx0_ref)

    @pl.when(t % 2 == 1)
    def _():
        _moba_step(*io, qa1_ref, qt1_ref, ka1_ref, qa0_ref, qt0_ref, ka0_ref,
                   s0_ref, mx0_ref, s1_ref, mx1_ref)


def _moba(proj, tables, batch, seq):
    kaug, eh = tables
    bs = MOBA_BLOCK
    nb = seq // bs
    n = MOBA_HEADS * batch
    qb, kb, vb, gb = (OFF_MQ // MOBA_DH, OFF_MK // MOBA_DH, OFF_MV // MOBA_DH, OFF_MG // MOBA_DH)
    first = lambda t: jnp.minimum(t, n - 1)
    third = lambda t: jnp.clip(t - 2, 0, n - 1)
    col = lambda base, which: pl.BlockSpec(
        (seq, MOBA_DH), lambda t: (which(t) % batch, base + which(t) // batch))
    pcol = lambda base, which: pl.BlockSpec(
        (None, seq, MOBA_DH), lambda t: (base + which(t) // batch, which(t) % batch, 0))
    return pl.pallas_call(
        _moba_kernel,
        out_shape=jax.ShapeDtypeStruct((batch * seq, MOBA_HEADS * MOBA_DH), BF16),
        grid=(n + 2,),
        in_specs=[pcol(qb, first), pcol(kb, first), pcol(vb, third), pcol(gb, third),
                  pl.BlockSpec((seq, MOBA_DH), lambda t: (0, 0)),
                  pl.BlockSpec((1, MOBA_AUG, MOBA_DH), lambda t: (first(t) // batch, 0, 0))],
        out_specs=col(0, third),
        scratch_shapes=[pltpu.VMEM((MOBA_DH + MOBA_AUG, seq), BF16)]
        + 2 * [pltpu.VMEM((nb, MOBA_DH, bs), BF16),
               pltpu.VMEM((MOBA_DH, seq), BF16),
               pltpu.VMEM((seq, 2 * MOBA_DH), BF16),
               pltpu.VMEM((nb * (nb + 1) // 2 * bs, bs), F32),
               pltpu.VMEM((nb, 1, bs), F32)],
        compiler_params=pltpu.CompilerParams(
            dimension_semantics=("arbitrary",), vmem_limit_bytes=VMEM_LIMIT),
        name="moba",
    )(proj, proj, proj, proj, kaug, eh)


def _out_proj_kernel(ret_ref, mo_ref, gr_ref, gm_ref, x_ref, wr_ref, wm_ref, wo_ref, nw_ref,
                     *out_refs, last):
    r = jnp.dot(ret_ref[...], wr_ref[...], preferred_element_type=F32)
    m = jnp.dot(mo_ref[...], wm_ref[...], preferred_element_type=F32)
    gr = jnp.concatenate([gr_ref[w] for w in range(gr_ref.shape[0])], axis=1)
    gm = jnp.concatenate([gm_ref[w] for w in range(gm_ref.shape[0])], axis=1)
    y = jax.nn.sigmoid(gr.astype(F32)) * r + jax.nn.sigmoid(gm.astype(F32)) * m
    out = x_ref[...] + jnp.dot(y.astype(BF16), wo_ref[...], preferred_element_type=F32)
    ms = jnp.mean(out * out, axis=-1, keepdims=True)
    normed = out * lax.rsqrt(ms + NORM_EPS) * nw_ref[...]
    if last:
        out_refs[0][...] = normed
    else:
        out_refs[0][...] = out
        out_refs[1][...] = normed.astype(BF16)


def _out_proj(ret_g, mo_g, proj, x2, w_ret_o, w_moba_o, w_out, norm_w3, layer, norm_layer, last):
    m = x2.shape[0]
    d = D_MODEL
    tile = lambda col: pl.BlockSpec((OUT_TM, d), lambda i: (i, col))
    slabs = lambda col: pl.BlockSpec((d // PROJ_LANES, OUT_TM, PROJ_LANES), lambda i: (col, i, 0))
    full = pl.BlockSpec((None, d, d), lambda i: (layer, 0, 0))
    x_shape = jax.ShapeDtypeStruct((m, d), F32)
    out_shape = x_shape if last else (x_shape, jax.ShapeDtypeStruct((m, d), BF16))
    out_specs = tile(0) if last else (tile(0), tile(0))
    return pl.pallas_call(
        functools.partial(_out_proj_kernel, last=last),
        out_shape=out_shape,
        grid=(m // OUT_TM,),
        in_specs=[tile(0), tile(0), slabs(OFF_GR // d), slabs(OFF_GM // d), tile(0),
                  full, full, full, pl.BlockSpec((None, 1, d), lambda i: (norm_layer, 0, 0))],
        out_specs=out_specs,
        compiler_params=pltpu.CompilerParams(
            dimension_semantics=("parallel",), vmem_limit_bytes=VMEM_LIMIT_OUT_PROJ),
        name="out_proj",
    )(ret_g, mo_g, proj, proj, x2, w_ret_o, w_moba_o, w_out, norm_w3)


def kernel(x, ln_w, w_in, ret_gn_w, w_ret_o, w_moba_o, w_out, final_norm_w):
    batch, seq, d = x.shape
    depth = w_in.shape[0]
    assert d == D_MODEL and w_in.shape[2] == D_IN
    assert seq % MOBA_BLOCK == 0 and seq % RET_CHUNK == 0 and (batch * seq) % IN_TM == 0
    ret_tables = _retention_tables()
    moba_tables = _moba_tables(seq)
    col_scale = _in_proj_col_scale()
    w_ret_b = w_ret_o.astype(BF16)
    w_moba_b = w_moba_o.astype(BF16)
    w_out_b = w_out.astype(BF16)
    ln_w3 = ln_w.reshape(depth, 1, d)
    gn_w3 = ret_gn_w.reshape(depth, 1, RET_HEADS * RET_DV)
    fn_w3 = final_norm_w.reshape(1, 1, d)
    x2 = x.reshape(batch * seq, d)
    h = _rmsnorm(x2, ln_w3, 0)
    for layer in range(depth):
        last = layer == depth - 1
        proj = _in_proj(h, w_in, col_scale, layer)
        ret_g = _retention(proj, gn_w3, layer, ret_tables, batch, seq)
        mo_g = _moba(proj, moba_tables, batch, seq)
        res = _out_proj(ret_g, mo_g, proj, x2, w_ret_b, w_moba_b, w_out_b,
                        fn_w3 if last else ln_w3, layer, 0 if last else layer + 1, last)
        if last:
            x2 = res
        else:
            x2, h = res
    return x2.reshape(batch, seq, d)
```

```python
import functools

import numpy as np
import jax
import jax.numpy as jnp
from jax import lax
from jax.experimental import pallas as pl
from jax.experimental.pallas import tpu as pltpu

D_MODEL = 1024
RET_HEADS = 4
RET_DK = 128
RET_DV = 256
RET_CHUNK = 256
RET_SEQS_PER_STEP = 4
MOBA_HEADS = 8
MOBA_DH = 128
MOBA_BLOCK = 256
MOBA_TOPK = 3
MOBA_EXP2_SCALE = MOBA_DH ** -0.5 * float(np.log2(np.e))
MOBA_AUG = 16
NORM_EPS = 1e-6

OFF_RQ, OFF_RK, OFF_RV, OFF_RG = 0, 512, 1024, 2048
OFF_MQ, OFF_MK, OFF_MV, OFF_MG = 3072, 4096, 5120, 6144
OFF_GR, OFF_GM = 7168, 8192
D_IN = 9216
PROJ_LANES = 128

NORM_TM = 2048
IN_TM = 2048
IN_TN = 1536
OUT_TM = 1024

V7X_VMEM_BYTES = 64 * 2 ** 20
V7X_SUBLANES = 8
VMEM_LIMIT = V7X_VMEM_BYTES * 3 // 4
VMEM_LIMIT_OUT_PROJ = V7X_VMEM_BYTES * 29 // 32

F32 = jnp.float32
BF16 = jnp.bfloat16

_NT = (((1,), (1,)), ((), ()))
_TN = (((0,), (0,)), ((), ()))


def _rmsnorm_kernel(x_ref, w_ref, o_ref):
    x = x_ref[...]
    ms = jnp.mean(x * x, axis=-1, keepdims=True)
    o_ref[...] = (x * lax.rsqrt(ms + NORM_EPS) * w_ref[...]).astype(o_ref.dtype)


def _rmsnorm(x2, ln_w3, layer):
    m, d = x2.shape
    return pl.pallas_call(
        _rmsnorm_kernel,
        out_shape=jax.ShapeDtypeStruct((m, d), BF16),
        grid=(m // NORM_TM,),
        in_specs=[pl.BlockSpec((NORM_TM, d), lambda i: (i, 0)),
                  pl.BlockSpec((None, 1, d), lambda i: (layer, 0, 0))],
        out_specs=pl.BlockSpec((NORM_TM, d), lambda i: (i, 0)),
        compiler_params=pltpu.CompilerParams(
            dimension_semantics=("parallel",), vmem_limit_bytes=VMEM_LIMIT),
        name="rmsnorm",
    )(x2, ln_w3)


def _in_proj_col_scale():
    cs = np.ones((1, D_IN), np.float32)
    cs[0, OFF_MQ:OFF_MQ + MOBA_HEADS * MOBA_DH] = MOBA_EXP2_SCALE
    cs[0, OFF_RG:OFF_RG + RET_HEADS * RET_DV] = 0.5
    cs[0, OFF_MG:OFF_MG + MOBA_HEADS * MOBA_DH] = 0.5
    return jnp.asarray(cs)


def _silu_from_half(hg):
    return hg + hg * jnp.tanh(hg)


def _in_proj_kernel(h_ref, w_ref, cs_ref, o_ref, wb_ref):
    @pl.when(pl.program_id(1) == 0)
    def _():
        wb_ref[...] = (w_ref[...] * cs_ref[...]).astype(BF16)

    res = jnp.dot(h_ref[...], wb_ref[...], preferred_element_type=F32).astype(o_ref.dtype)
    for c in range(o_ref.shape[0]):
        o_ref[c] = res[:, c * PROJ_LANES:(c + 1) * PROJ_LANES]


def _in_proj(h, w_in, col_scale, layer):
    m = h.shape[0]
    return pl.pallas_call(
        _in_proj_kernel,
        out_shape=jax.ShapeDtypeStruct((D_IN // PROJ_LANES, m, PROJ_LANES), BF16),
        grid=(D_IN // IN_TN, m // IN_TM),
        in_specs=[
            pl.BlockSpec((IN_TM, D_MODEL), lambda j, i: (i, 0)),
            pl.BlockSpec((None, D_MODEL, IN_TN), lambda j, i: (layer, 0, j)),
            pl.BlockSpec((1, IN_TN), lambda j, i: (0, j)),
        ],
        out_specs=pl.BlockSpec((IN_TN // PROJ_LANES, IN_TM, PROJ_LANES), lambda j, i: (j, i, 0)),
        scratch_shapes=[pltpu.VMEM((D_MODEL, IN_TN), BF16)],
        compiler_params=pltpu.CompilerParams(
            dimension_semantics=("parallel", "arbitrary"),
            vmem_limit_bytes=VMEM_LIMIT,
        ),
        name="in_proj",
    )(h, w_in, col_scale)


def _retention_tables():
    c = RET_CHUNK
    hh = np.arange(RET_HEADS, dtype=np.float64)
    log_g = np.log1p(-np.exp2(-5.0 - hh))
    n = np.arange(c, dtype=np.float64)
    rel = n[:, None] - n[None, :]
    scale = RET_DK ** -0.5
    dec = np.where(rel[None] >= 0, np.exp(np.maximum(rel, 0.0)[None] * log_g[:, None, None]), 0.0) * scale
    xi = np.exp((n + 1.0)[None, :] * log_g[:, None])
    zeta = np.exp((c - 1.0 - n)[None, :] * log_g[:, None]) * scale
    gch = np.exp(c * log_g)
    xi_b = np.broadcast_to(xi[:, :, None], (RET_HEADS, c, RET_DV))
    zeta_b = np.broadcast_to(zeta[:, :, None], (RET_HEADS, c, RET_DK))
    gch_b = np.broadcast_to(gch[:, None, None], (RET_HEADS, 1, RET_DV))
    f = lambda a: jnp.asarray(np.ascontiguousarray(a), dtype=F32)
    return f(dec), f(xi_b), f(zeta_b), f(gch_b)


def _retention_kernel(q_ref, k_ref, v_ref, g_ref, gnw_ref, dec_ref, xi_ref, zeta_ref, gch_ref, o_ref):
    c = RET_CHUNK
    seq = q_ref.shape[0] // RET_SEQS_PER_STEP
    n_chunks = seq // c
    states = [None] * RET_SEQS_PER_STEP
    for ci in range(n_chunks):
        for u in range(RET_SEQS_PER_STEP):
            rows = slice(u * seq + ci * c, u * seq + (ci + 1) * c)
            st = states[u]
            qc = q_ref[rows, :]
            kc = k_ref[rows, :]
            vc = jnp.concatenate([v_ref[w, rows, :] for w in range(v_ref.shape[0])], axis=1)
            inner = lax.dot_general(qc, kc, _NT, preferred_element_type=F32) * dec_ref[0]
            if st is None:
                o = jnp.dot(inner.astype(BF16), vc, preferred_element_type=F32)
            else:
                qx = (qc.astype(F32) * xi_ref[0, :, 0:RET_DK]).astype(BF16)
                lhs = jnp.concatenate([inner.astype(BF16), qx], axis=1)
                rhs = jnp.concatenate([vc, st.astype(BF16)], axis=0)
                o = jnp.dot(lhs, rhs, preferred_element_type=F32)
            if ci + 1 < n_chunks:
                kz = (kc.astype(F32) * zeta_ref[0]).astype(BF16)
                kv = lax.dot_general(kz, vc, _TN, preferred_element_type=F32)
                states[u] = kv if st is None else st * gch_ref[0] + kv
            mu = jnp.mean(o, axis=-1, keepdims=True)
            d = o - mu
            var = jnp.mean(d * d, axis=-1, keepdims=True)
            y = d * lax.rsqrt(var + NORM_EPS) * gnw_ref[...]
            g = jnp.concatenate([g_ref[w, rows, :] for w in range(g_ref.shape[0])], axis=1)
            o_ref[rows, :] = (y * _silu_from_half(g.astype(F32))).astype(o_ref.dtype)


def _retention(proj, gn_w3, layer, tables, batch, seq):
    dec, xi_b, zeta_b, gch_b = tables
    c = RET_CHUNK
    qb, kb = OFF_RQ // RET_DK, OFF_RK // RET_DK
    vb, gb = OFF_RV // RET_DV, OFF_RG // RET_DV
    n_sub = RET_SEQS_PER_STEP
    assert batch % n_sub == 0
    rows = n_sub * seq
    return pl.pallas_call(
        _retention_kernel,
        out_shape=jax.ShapeDtypeStruct((batch * seq, RET_HEADS * RET_DV), BF16),
        grid=(RET_HEADS, batch // n_sub),
        in_specs=[
            pl.BlockSpec((None, rows, RET_DK), lambda h, b: (qb + h, b, 0)),
            pl.BlockSpec((None, rows, RET_DK), lambda h, b: (kb + h, b, 0)),
            pl.BlockSpec((RET_DV // PROJ_LANES, rows, PROJ_LANES), lambda h, b: (vb + h, b, 0)),
            pl.BlockSpec((RET_DV // PROJ_LANES, rows, PROJ_LANES), lambda h, b: (gb + h, b, 0)),
            pl.BlockSpec((None, 1, RET_DV), lambda h, b: (layer, 0, h)),
            pl.BlockSpec((1, c, c), lambda h, b: (h, 0, 0)),
            pl.BlockSpec((1, c, RET_DV), lambda h, b: (h, 0, 0)),
            pl.BlockSpec((1, c, RET_DK), lambda h, b: (h, 0, 0)),
            pl.BlockSpec((1, 1, RET_DV), lambda h, b: (h, 0, 0)),
        ],
        out_specs=pl.BlockSpec((rows, RET_DV), lambda h, b: (b, h)),
        compiler_params=pltpu.CompilerParams(
            dimension_semantics=("parallel", "parallel"), vmem_limit_bytes=VMEM_LIMIT),
        name="retention",
    )(proj, proj, proj, proj, gn_w3, dec, xi_b, zeta_b, gch_b)


def _bf16_terms(x, n):
    terms, rest = [], np.asarray(x, np.float64)
    for _ in range(n):
        t = rest.astype(np.float32).astype(BF16).astype(np.float64)
        terms.append(t)
        rest = rest - t
    return terms


def _moba_tables(seq):
    nb = seq // MOBA_BLOCK
    n_terms = 3
    col_pos, col_start = nb, nb + n_terms
    assert nb < MOBA_AUG and col_start + n_terms <= MOBA_DH
    assert MOBA_BLOCK <= 256 and nb <= 256
    pos = np.arange(seq)
    kaug = np.zeros((seq, MOBA_DH), np.float32)
    kaug[pos, pos // MOBA_BLOCK] = 1.0
    kaug[:, col_pos:col_pos + n_terms] = (pos % MOBA_BLOCK)[:, None]
    kaug[:, col_start:col_start + n_terms] = (pos // MOBA_BLOCK * MOBA_BLOCK)[:, None]
    slopes = np.exp2(-8.0 * (np.arange(MOBA_HEADS, dtype=np.float64) + 1.0) / MOBA_HEADS)
    a_terms = _bf16_terms(slopes * np.log2(np.e), n_terms)
    eh = np.zeros((MOBA_HEADS, MOBA_AUG, MOBA_DH), np.float32)
    eh[:, np.arange(nb), np.arange(nb)] = 1.0
    for t, a in enumerate(a_terms):
        eh[:, nb, col_pos + t] = a
        eh[:, nb, col_start + t] = a
    return jnp.asarray(kaug, dtype=BF16), jnp.asarray(eh, dtype=BF16)


def _moba_step(q_ref, k_ref, v_ref, g_ref, kaug_ref, eh_ref, o_ref, vt_ref,
               qa_w, qt_w, ka_w, qa_r, qt_r, ka_r, s_w, mx_w, s_r, mx_r):
    bs = MOBA_BLOCK
    dh = MOBA_DH
    seq = kaug_ref.shape[0]
    nb = seq // bs
    blk = lambda j: slice(j * bs, (j + 1) * bs)
    pair = lambda i, j: slice((i * (i + 1) // 2 + j) * bs, (i * (i + 1) // 2 + j + 1) * bs)
    masked = -(2.0 ** 127)

    jidx = lax.broadcasted_iota(jnp.int32, (nb, bs), 0)
    kpos = lax.broadcasted_iota(jnp.int32, (bs, bs), 0)
    qpos = lax.broadcasted_iota(jnp.int32, (bs, bs), 1)
    causal = kpos <= qpos
    aug_tail = (lax.broadcasted_iota(jnp.int32, (MOBA_AUG - nb, bs), 0) == 0).astype(F32)
    eh = eh_ref[0]

    def query_side(mask_rows):
        aug = jnp.concatenate([mask_rows, aug_tail], axis=0).astype(BF16)
        return lax.dot_general(eh, aug, _TN, preferred_element_type=F32).astype(BF16)

    qt_w[...] = q_ref[...].T
    ka_w[:, 0:dh] = k_ref[...]
    km = jnp.concatenate(
        [jnp.sum(k_ref[blk(j), :].astype(F32), axis=0, keepdims=True) for j in range(nb)], axis=0) * (1.0 / bs)
    km_hi = km.astype(BF16)
    km_lo = (km - km_hi.astype(F32)).astype(BF16)
    km_hl = jnp.concatenate([km_hi, km_lo], axis=0)
    qt_right_all = query_side(jnp.zeros((nb, bs), F32))

    def setup_block(i):
        qt = qt_w[:, blk(i)]
        if i > MOBA_TOPK:
            gate2 = jnp.dot(km_hl, qt, preferred_element_type=F32)
            gate = gate2[:nb] + gate2[nb:]
            rank = jnp.zeros((nb, bs), F32)
            for jj in range(i):
                row = gate[jj:jj + 1, :]
                rank = rank + ((row > gate) | ((row == gate) & (jj < jidx))).astype(F32)
            keep = (rank < float(MOBA_TOPK)) | (jidx >= i)
            qt_right = query_side(jnp.where(keep, 0.0, masked))
        else:
            qt_right = qt_right_all
        qa_w[i] = qt_right

    vt_ref[0:dh, :] = v_ref[...].T

    def score_pass(i):
        qt_aug = jnp.concatenate([qt_r[:, blk(i)], qa_r[i]], axis=0)
        mx = None
        for j in range(i + 1):
            s = jnp.dot(ka_r[blk(j), :], qt_aug, preferred_element_type=F32)
            if j == i:
                s = jnp.where(causal, s, -jnp.inf)
            s_w[pair(i, j), :] = s
            mj = jnp.max(s.reshape(bs // V7X_SUBLANES, V7X_SUBLANES, bs), axis=0)
            mx = mj if mx is None else jnp.maximum(mx, mj)
        mx_w[i] = jnp.max(mx, axis=0, keepdims=True)

    def value_pass(i):
        mx = mx_r[i]
        acc = None
        for j in range(i + 1):
            if j == i:
                hb = bs // 2
                r0 = pair(i, j).start
                top = jnp.exp2(s_r[r0:r0 + hb, :] - mx).astype(BF16)
                right = jnp.exp2(s_r[r0 + hb:r0 + bs, hb:bs] - mx[:, hb:bs]).astype(BF16)
                p = jnp.concatenate([top, jnp.concatenate([jnp.zeros((hb, hb), BF16), right], axis=1)], axis=0)
            else:
                p = jnp.exp2(s_r[pair(i, j), :] - mx).astype(BF16)
            pv = jnp.dot(vt_ref[:, blk(j)], p, preferred_element_type=F32)
            acc = pv if acc is None else acc + pv
        out = (acc[:dh] / acc[dh:dh + 1]).T
        gate = _silu_from_half(g_ref[blk(i), :].astype(F32))
        o_ref[blk(i), :] = (out * gate).astype(o_ref.dtype)

    for i in reversed(range(nb)):
        score_pass(i)
        value_pass(i)
        setup_block(i)


def _moba_kernel(q_ref, k_ref, v_ref, g_ref, kaug_ref, eh_ref, o_ref, vt_ref,
                 qa0_ref, qt0_ref, ka0_ref, s0_ref, mx0_ref, qa1_ref, qt1_ref, ka1_ref, s1_ref, mx1_ref):
    t = pl.program_id(0)
    io = (q_ref, k_ref, v_ref, g_ref, kaug_ref, eh_ref, o_ref, vt_ref)

    @pl.when(t == 0)
    def _():
        for ref in (qa1_ref, qt1_ref, ka1_ref, s0_ref, mx0_ref):
            ref[...] = jnp.zeros(ref.shape, ref.dtype)
        for ka_ref in (ka0_ref, ka1_ref):
            ka_ref[:, MOBA_DH:2 * MOBA_DH] = kaug_ref[...]
        tail_shape = (MOBA_AUG, vt_ref.shape[1])
        vt_ref[MOBA_DH:, :] = (lax.broadcasted_iota(jnp.int32, tail_shape, 0) == 0).astype(BF16)

    @pl.when(t % 2 == 0)
    def _():
        _moba_step(*io, qa0_ref, qt0_ref, ka0_ref, qa1_ref, qt1_ref, ka1_ref,
                   s1_ref, mx1_ref, s0_ref, mx0_ref)

    @pl.when(t % 2 == 1)
    def _():
        _moba_step(*io, qa1_ref, qt1_ref, ka1_ref, qa0_ref, qt0_ref, ka0_ref,
                   s0_ref, mx0_ref, s1_ref, mx1_ref)


def _moba(proj, tables, batch, seq):
    kaug, eh = tables
    bs = MOBA_BLOCK
    nb = seq // bs
    n = MOBA_HEADS * batch
    qb, kb, vb, gb = (OFF_MQ // MOBA_DH, OFF_MK // MOBA_DH, OFF_MV // MOBA_DH, OFF_MG // MOBA_DH)
    first = lambda t: jnp.minimum(t, n - 1)
    third = lambda t: jnp.clip(t - 2, 0, n - 1)
    col = lambda base, which: pl.BlockSpec(
        (seq, MOBA_DH), lambda t: (which(t) % batch, base + which(t) // batch))
    pcol = lambda base, which: pl.BlockSpec(
        (None, seq, MOBA_DH), lambda t: (base + which(t) // batch, which(t) % batch, 0))
    return pl.pallas_call(
        _moba_kernel,
        out_shape=jax.ShapeDtypeStruct((batch * seq, MOBA_HEADS * MOBA_DH), BF16),
        grid=(n + 2,),
        in_specs=[pcol(qb, first), pcol(kb, first), pcol(vb, third), pcol(gb, third),
                  pl.BlockSpec((seq, MOBA_DH), lambda t: (0, 0)),
                  pl.BlockSpec((1, MOBA_AUG, MOBA_DH), lambda t: (first(t) // batch, 0, 0))],
        out_specs=col(0, third),
        scratch_shapes=[pltpu.VMEM((MOBA_DH + MOBA_AUG, seq), BF16)]
        + 2 * [pltpu.VMEM((nb, MOBA_DH, bs), BF16),
               pltpu.VMEM((MOBA_DH, seq), BF16),
               pltpu.VMEM((seq, 2 * MOBA_DH), BF16),
               pltpu.VMEM((nb * (nb + 1) // 2 * bs, bs), F32),
               pltpu.VMEM((nb, 1, bs), F32)],
        compiler_params=pltpu.CompilerParams(
            dimension_semantics=("arbitrary",), vmem_limit_bytes=VMEM_LIMIT),
        name="moba",
    )(proj, proj, proj, proj, kaug, eh)


def _out_proj_kernel(ret_ref, mo_ref, gr_ref, gm_ref, x_ref, wr_ref, wm_ref, wo_ref, nw_ref,
                     *out_refs, last):
    r = jnp.dot(ret_ref[...], wr_ref[...], preferred_element_type=F32)
    m = jnp.dot(mo_ref[...], wm_ref[...], preferred_element_type=F32)
    gr = jnp.concatenate([gr_ref[w] for w in range(gr_ref.shape[0])], axis=1)
    gm = jnp.concatenate([gm_ref[w] for w in range(gm_ref.shape[0])], axis=1)
    y = jax.nn.sigmoid(gr.astype(F32)) * r + jax.nn.sigmoid(gm.astype(F32)) * m
    out = x_ref[...] + jnp.dot(y.astype(BF16), wo_ref[...], preferred_element_type=F32)
    ms = jnp.mean(out * out, axis=-1, keepdims=True)
    normed = out * lax.rsqrt(ms + NORM_EPS) * nw_ref[...]
    if last:
        out_refs[0][...] = normed
    else:
        out_refs[0][...] = out
        out_refs[1][...] = normed.astype(BF16)


def _out_proj(ret_g, mo_g, proj, x2, w_ret_o, w_moba_o, w_out, norm_w3, layer, norm_layer, last):
    m = x2.shape[0]
    d = D_MODEL
    tile = lambda col: pl.BlockSpec((OUT_TM, d), lambda i: (i, col))
    slabs = lambda col: pl.BlockSpec((d // PROJ_LANES, OUT_TM, PROJ_LANES), lambda i: (col, i, 0))
    full = pl.BlockSpec((d, d), lambda i: (0, 0))
    bf16_layer = lambda w: w[layer].astype(BF16)
    x_shape = jax.ShapeDtypeStruct((m, d), F32)
    out_shape = x_shape if last else (x_shape, jax.ShapeDtypeStruct((m, d), BF16))
    out_specs = tile(0) if last else (tile(0), tile(0))
    return pl.pallas_call(
        functools.partial(_out_proj_kernel, last=last),
        out_shape=out_shape,
        grid=(m // OUT_TM,),
        in_specs=[tile(0), tile(0), slabs(OFF_GR // d), slabs(OFF_GM // d), tile(0),
                  full, full, full, pl.BlockSpec((None, 1, d), lambda i: (norm_layer, 0, 0))],
        out_specs=out_specs,
        compiler_params=pltpu.CompilerParams(
            dimension_semantics=("parallel",), vmem_limit_bytes=VMEM_LIMIT_OUT_PROJ,
            allow_input_fusion=[False] * 5 + [True] * 3 + [False]),
        name="out_proj",
    )(ret_g, mo_g, proj, proj, x2, bf16_layer(w_ret_o), bf16_layer(w_moba_o), bf16_layer(w_out), norm_w3)


def kernel(x, ln_w, w_in, ret_gn_w, w_ret_o, w_moba_o, w_out, final_norm_w):
    batch, seq, d = x.shape
    depth = w_in.shape[0]
    assert d == D_MODEL and w_in.shape[2] == D_IN
    assert seq % MOBA_BLOCK == 0 and seq % RET_CHUNK == 0 and (batch * seq) % IN_TM == 0
    ret_tables = _retention_tables()
    moba_tables = _moba_tables(seq)
    col_scale = _in_proj_col_scale()
    ln_w3 = ln_w.reshape(depth, 1, d)
    gn_w3 = ret_gn_w.reshape(depth, 1, RET_HEADS * RET_DV)
    fn_w3 = final_norm_w.reshape(1, 1, d)
    x2 = x.reshape(batch * seq, d)
    h = _rmsnorm(x2, ln_w3, 0)
    for layer in range(depth):
        last = layer == depth - 1
        proj = _in_proj(h, w_in, col_scale, layer)
        ret_g = _retention(proj, gn_w3, layer, ret_tables, batch, seq)
        mo_g = _moba(proj, moba_tables, batch, seq)
        res = _out_proj(ret_g, mo_g, proj, x2, w_ret_o, w_moba_o, w_out,
                        fn_w3 if last else ln_w3, layer, 0 if last else layer + 1, last)
        if last:
            x2 = res
        else:
            x2, h = res
    return x2.reshape(batch, seq, d)
```
